```python
import math, functools
import jax, jax.numpy as jnp
from jax import lax
import numpy as np

D_MODEL = 4096
BATCH = 32
SEQ = 256
DEPTH = 4
DEC_BATCH = 4
DEC_SEQ = 4096
PAST_LEN = 256

GRID_W = 64
N_MIXERS = 2
N_GLA = (DEPTH + 1) // 2
N_FNET = DEPTH // 2
GLA_HEADS = 8
GLA_HEAD_K = D_MODEL // (2 * GLA_HEADS)
GLA_HEAD_V = D_MODEL // GLA_HEADS
GLA_KEY = GLA_HEADS * GLA_HEAD_K
GLA_VAL = GLA_HEADS * GLA_HEAD_V
GATE_RANK = 16
GATE_NORM = 16.0
CHUNK = 32
F_GROUPS = 4
D_FF = 4 * D_MODEL
EPS = 1e-6

kernel_name = 'hybrid_gla_fnet_diffusion_step'

F32 = jnp.float32


def rmsnorm(x, g):
    xf = x.astype(F32)
    y = xf * lax.rsqrt(jnp.mean(xf * xf, axis=-1, keepdims=True) + EPS)
    return (y * g.astype(F32)).astype(x.dtype)


def adaln_params(cond, w, b):
    m = jnp.dot(jax.nn.silu(cond), w) + b
    return jnp.split(m[:, None, :], 6, axis=-1)


def pos_embed_2d(n_tok, dtype):
    rows = n_tok // GRID_W
    quarter = D_MODEL // 4
    omega = 1.0 / (10000.0 ** (jnp.arange(quarter, dtype=F32) / quarter))
    er = jnp.arange(rows, dtype=F32)[:, None] * omega
    ec = jnp.arange(GRID_W, dtype=F32)[:, None] * omega
    er = jnp.concatenate([jnp.sin(er), jnp.cos(er)], axis=-1)
    ec = jnp.concatenate([jnp.sin(ec), jnp.cos(ec)], axis=-1)
    half = D_MODEL // 2
    emb = jnp.concatenate([jnp.broadcast_to(er[:, None, :], (rows, GRID_W, half)),
                           jnp.broadcast_to(ec[None, :, :], (rows, GRID_W, half))], axis=-1)
    return emb.reshape(rows * GRID_W, D_MODEL).astype(dtype)


def gla_chunk_scan(q, k, v, g, s0):
    bsz, t, h, _ = q.shape
    dv = v.shape[-1]
    n = t // CHUNK

    def to_chunks(a):
        return a.astype(F32).reshape(bsz, n, CHUNK, h, a.shape[-1]).transpose(1, 0, 3, 2, 4)

    qs, ks, vs, gs = to_chunks(q), to_chunks(k), to_chunks(v), to_chunks(g)
    mask = jnp.tril(jnp.ones((CHUNK, CHUNK), dtype=bool))[:, :, None]

    def step(s, inp):
        qc, kc, vc, gc = inp
        b = jnp.cumsum(gc, axis=2)
        o_inter = jnp.einsum('bhtk,bhkv->bhtv', qc * jnp.exp(b), s)
        diff = jnp.where(mask, b[:, :, :, None, :] - b[:, :, None, :, :], -jnp.inf)
        att = jnp.einsum('bhtk,bhsk,bhtsk->bhts', qc, kc, jnp.exp(diff))
        o_intra = jnp.einsum('bhts,bhsv->bhtv', att, vc)
        b_last = b[:, :, -1:, :]
        s_new = (jnp.exp(b_last[:, :, 0, :])[..., None] * s
                 + jnp.einsum('bhtk,bhtv->bhkv', kc * jnp.exp(b_last - b), vc))
        return s_new, o_inter + o_intra

    s_fin, o = lax.scan(step, s0.astype(F32), (qs, ks, vs, gs))
    o = o.transpose(1, 0, 3, 2, 4).reshape(bsz, t, h, dv)
    return o, s_fin


def gla_mixer(h, w_in, wa1_f, wa2_f, ba_f, wa1_b, wa2_b, ba_b, on_g, w_out, s0_f, s0_b):
    bsz, t, _ = h.shape
    z = jnp.dot(h, w_in)
    q, k, v, r = jnp.split(z, [GLA_KEY, 2 * GLA_KEY, 2 * GLA_KEY + GLA_VAL], axis=-1)

    def heads(a, d):
        return a.reshape(bsz, t, GLA_HEADS, d)

    q = heads(q, GLA_HEAD_K) * (GLA_HEAD_K ** -0.5)
    k = heads(k, GLA_HEAD_K)
    v = heads(v, GLA_HEAD_V)

    def log_gate(wa1, wa2, ba):
        zz = jnp.dot(jnp.dot(h, wa1), wa2) + ba
        return heads(jax.nn.log_sigmoid(zz.astype(F32)) / GATE_NORM, GLA_HEAD_K)

    def flip(a):
        return jnp.flip(a, axis=1)

    o_f, s_f = gla_chunk_scan(q, k, v, log_gate(wa1_f, wa2_f, ba_f), s0_f)
    o_b, s_b = gla_chunk_scan(flip(q), flip(k), flip(v), flip(log_gate(wa1_b, wa2_b, ba_b)), s0_b)
    o = o_f + flip(o_b)
    o = o * lax.rsqrt(jnp.mean(o * o, axis=-1, keepdims=True) + EPS) * on_g.astype(F32)
    o = o * jax.nn.silu(heads(r, GLA_HEAD_V).astype(F32))
    y = jnp.dot(o.reshape(bsz, t, GLA_VAL).astype(h.dtype), w_out)
    return y, s_f, s_b


def fourier_mixer(h, w):
    bsz, t, d = h.shape
    hf = h.astype(F32).reshape(bsz, t, F_GROUPS, d // F_GROUPS)
    y = jnp.fft.fft2(hf, axes=(1, 3), norm='ortho').real
    return jnp.dot(y.reshape(bsz, t, d).astype(h.dtype), w)


def sq_relu_mlp(h, w1, w2):
    return jnp.dot(jnp.square(jax.nn.relu(jnp.dot(h, w1))), w2)


def setup_inputs(seed: int = 0) -> dict:
    key = jax.random.key(seed)
    ks = jax.random.split(key, 24)

    def nrm(k, shape, scale):
        return jax.random.normal(k, shape, F32) * scale

    d = D_MODEL
    st_shape = (DEC_BATCH, N_GLA, GLA_HEADS, GLA_HEAD_K, GLA_HEAD_V)
    return {
        'x_prompt': nrm(ks[0], (BATCH, SEQ, d), 1.0),
        'x_sample': nrm(ks[1], (DEC_BATCH, DEC_SEQ, d), 1.0),
        'state_fwd': nrm(ks[2], st_shape, 0.5),
        'state_bwd': nrm(ks[3], st_shape, 0.5),
        'c': nrm(ks[4], (DEC_BATCH, d), 1.0),
        'c_ctx': nrm(ks[5], (d,), 1.0),
        'norm1_g': 1.0 + nrm(ks[6], (DEPTH, d), 0.02),
        'norm2_g': 1.0 + nrm(ks[7], (DEPTH, d), 0.02),
        'w_mod': nrm(ks[8], (DEPTH, d, 6 * d), 0.5 * d ** -0.5),
        'b_mod': nrm(ks[9], (DEPTH, 6 * d), 0.01),
        'gla_w_in': nrm(ks[10], (N_GLA, d, 2 * GLA_KEY + 2 * GLA_VAL), d ** -0.5),
        'gla_wa1_f': nrm(ks[11], (N_GLA, d, GATE_RANK), d ** -0.5),
        'gla_wa2_f': nrm(ks[12], (N_GLA, GATE_RANK, GLA_KEY), GATE_RANK ** -0.5),
        'gla_ba_f': nrm(ks[13], (N_GLA, GLA_KEY), 0.1),
        'gla_wa1_b': nrm(ks[14], (N_GLA, d, GATE_RANK), d ** -0.5),
        'gla_wa2_b': nrm(ks[15], (N_GLA, GATE_RANK, GLA_KEY), GATE_RANK ** -0.5),
        'gla_ba_b': nrm(ks[16], (N_GLA, GLA_KEY), 0.1),
        'gla_onorm_g': 1.0 + nrm(ks[17], (N_GLA, GLA_HEAD_V), 0.02),
        'gla_w_out': nrm(ks[18], (N_GLA, GLA_VAL, d), GLA_VAL ** -0.5),
        'fnet_w': nrm(ks[19], (N_FNET, d, d), d ** -0.5),
        'mlp_w1': nrm(ks[20], (DEPTH, d, D_FF), d ** -0.5),
        'mlp_w2': nrm(ks[21], (DEPTH, D_FF, d), D_FF ** -0.5),
        'final_g': 1.0 + nrm(ks[22], (d,), 0.02),
    }


def reference(x_prompt, x_sample, state_fwd, state_bwd, c, c_ctx, norm1_g, norm2_g, w_mod, b_mod,
              gla_w_in, gla_wa1_f, gla_wa2_f, gla_ba_f, gla_wa1_b, gla_wa2_b, gla_ba_b,
              gla_onorm_g, gla_w_out, fnet_w, mlp_w1, mlp_w2, final_g):
    xp = x_prompt
    xl = x_sample + pos_embed_2d(x_sample.shape[1], x_sample.dtype)[None]
    ctx_cond = c_ctx[None, :]
    zero_state = jnp.zeros((xp.shape[0], GLA_HEADS, GLA_HEAD_K, GLA_HEAD_V), F32)
    new_f = []
    new_b = []
    for i in range(DEPTH):
        ps = adaln_params(ctx_cond, w_mod[i], b_mod[i])
        pl = adaln_params(c, w_mod[i], b_mod[i])
        hp = rmsnorm(xp, norm1_g[i]) * (1 + ps[1]) + ps[0]
        hl = rmsnorm(xl, norm1_g[i]) * (1 + pl[1]) + pl[0]
        j = i // N_MIXERS
        if i % N_MIXERS == 0:
            gw = (gla_w_in[j], gla_wa1_f[j], gla_wa2_f[j], gla_ba_f[j],
                  gla_wa1_b[j], gla_wa2_b[j], gla_ba_b[j], gla_onorm_g[j], gla_w_out[j])
            yp, sf, sb = gla_mixer(hp, *gw, zero_state, zero_state)
            yl, _, _ = gla_mixer(hl, *gw, state_fwd[:, j], state_bwd[:, j])
            new_f.append(sf.astype(xp.dtype))
            new_b.append(sb.astype(xp.dtype))
        else:
            yp = fourier_mixer(hp, fnet_w[j])
            yl = fourier_mixer(hl, fnet_w[j])
        xp = xp + ps[2] * yp
        xl = xl + pl[2] * yl
        hp = rmsnorm(xp, norm2_g[i]) * (1 + ps[4]) + ps[3]
        hl = rmsnorm(xl, norm2_g[i]) * (1 + pl[4]) + pl[3]
        xp = xp + ps[5] * sq_relu_mlp(hp, mlp_w1[i], mlp_w2[i])
        xl = xl + pl[5] * sq_relu_mlp(hl, mlp_w1[i], mlp_w2[i])
    y_prompt = rmsnorm(xp, final_g)
    y_sample = rmsnorm(xl, final_g)
    new_state_fwd = jnp.stack(new_f, axis=1)
    new_state_bwd = jnp.stack(new_b, axis=1)
    return (y_prompt, y_sample, new_state_fwd, new_state_bwd)
```

```python
import functools
import math

import numpy as np
import jax
import jax.numpy as jnp
from jax import lax
from jax.experimental import pallas as pl
from jax.experimental.pallas import tpu as pltpu

F32 = jnp.float32
BF16 = jnp.bfloat16

EPS = 1e-6
GLA_HEADS = 8
GATE_NORM = 16.0
F_GROUPS = 4
GRID_W = 64
N_MOD = 6
GLA_CHUNK = 128
A1_WIDTH = 128


def _pick(dim, pref):
    t = min(dim, pref)
    while dim % t:
        t //= 2
    return t


def _cparams(semantics, vmem_mib):
    return pltpu.CompilerParams(dimension_semantics=semantics,
                                vmem_limit_bytes=int(vmem_mib) << 20)


class _Rows:
    def __init__(self, n_p, t_l, n_cond):
        self.n_p, self.t_l, self.n_cond = n_p, t_l, n_cond

    def cond(self, row0):
        return jnp.where(row0 < self.n_p, 0, 1 + (row0 - self.n_p) // self.t_l)

    def mod_row(self, layer, which, row0):
        return (layer * self.n_cond + self.cond(row0)) * N_MOD + which


def _mod_kernel(c_ref, w_ref, b_ref, o_ref):
    c = c_ref[...]
    a = (c * jax.nn.sigmoid(c)).astype(BF16)
    o_ref[...] = jnp.dot(a, w_ref[...].astype(BF16), preferred_element_type=F32) + b_ref[...]


def _modulation(cond, w_mod, b_mod):
    depth, d, n = w_mod.shape
    r = cond.shape[0]
    tn = _pick(n, 512)
    return pl.pallas_call(
        _mod_kernel,
        grid=(depth, n // tn),
        in_specs=[pl.BlockSpec((r, d), lambda l, j: (0, 0)),
                  pl.BlockSpec((None, d, tn), lambda l, j: (l, 0, j)),
                  pl.BlockSpec((None, 1, tn), lambda l, j: (l, 0, j))],
        out_specs=pl.BlockSpec((None, r, tn), lambda l, j: (l, 0, j)),
        out_shape=jax.ShapeDtypeStruct((depth, r, n), F32),
        compiler_params=_cparams(("parallel", "parallel"), 40),
        name="modulation",
    )(cond, w_mod, b_mod.reshape(depth, 1, n))


def _embed_kernel(xp_ref, xl_ref, pos_ref, o_ref, *, n_pblk):
    i = pl.program_id(0)

    @pl.when(i < n_pblk)
    def _():
        o_ref[...] = xp_ref[...]

    @pl.when(i >= n_pblk)
    def _():
        o_ref[...] = xl_ref[...] + pos_ref[...]


def _embed(xp2, xl2, pos):
    n_p, d = xp2.shape
    n_l = xl2.shape[0]
    t_l = pos.shape[0]
    tm = _pick(math.gcd(n_p, t_l), 256)
    n_pblk, n_posblk = n_p // tm, t_l // tm
    return pl.pallas_call(
        functools.partial(_embed_kernel, n_pblk=n_pblk),
        grid=((n_p + n_l) // tm,),
        in_specs=[pl.BlockSpec((tm, d), lambda i: (jnp.minimum(i, n_pblk - 1), 0)),
                  pl.BlockSpec((tm, d), lambda i: (jnp.maximum(i - n_pblk, 0), 0)),
                  pl.BlockSpec((tm, d), lambda i: (jnp.maximum(i - n_pblk, 0) % n_posblk, 0))],
        out_specs=pl.BlockSpec((tm, d), lambda i: (i, 0)),
        out_shape=jax.ShapeDtypeStruct((n_p + n_l, d), F32),
        compiler_params=_cparams(("parallel",), 48),
        name="embed",
    )(xp2, xl2, pos)


def _norm_mod_kernel(x_ref, g_ref, scale_ref, shift_ref, o_ref):
    x = x_ref[...]
    y = x * lax.rsqrt(jnp.mean(x * x, axis=-1, keepdims=True) + EPS)
    o_ref[...] = ((y * g_ref[...]) * (1.0 + scale_ref[...]) + shift_ref[...]).astype(o_ref.dtype)


def _norm_mod(x, g, mods3, rows, layer, which_shift):
    m, d = x.shape
    tm = _pick(math.gcd(rows.n_p, rows.t_l), 512)
    mod_spec = lambda which: pl.BlockSpec(
        (None, 1, d), lambda i: (rows.mod_row(layer, which, i * tm), 0, 0))
    return pl.pallas_call(
        _norm_mod_kernel,
        grid=(m // tm,),
        in_specs=[pl.BlockSpec((tm, d), lambda i: (i, 0)),
                  pl.BlockSpec((1, d), lambda i: (0, 0)),
                  mod_spec(which_shift + 1), mod_spec(which_shift)],
        out_specs=pl.BlockSpec((tm, d), lambda i: (i, 0)),
        out_shape=jax.ShapeDtypeStruct((m, d), BF16),
        compiler_params=_cparams(("parallel",), 48),
        name="norm_mod",
    )(x, g.reshape(1, d), mods3, mods3)


def _final_norm_kernel(x_ref, g_ref, o_ref):
    x = x_ref[...]
    y = x * lax.rsqrt(jnp.mean(x * x, axis=-1, keepdims=True) + EPS)
    o_ref[...] = y * g_ref[...]


def _final_norm(x, g, row0, nrows):
    d = x.shape[1]
    tm = _pick(math.gcd(row0, nrows) if row0 else nrows, 512)
    blk0 = row0 // tm
    return pl.pallas_call(
        _final_norm_kernel,
        grid=(nrows // tm,),
        in_specs=[pl.BlockSpec((tm, d), lambda i: (blk0 + i, 0)),
                  pl.BlockSpec((1, d), lambda i: (0, 0))],
        out_specs=pl.BlockSpec((tm, d), lambda i: (i, 0)),
        out_shape=jax.ShapeDtypeStruct((nrows, d), F32),
        compiler_params=_cparams(("parallel",), 48),
        name="final_norm",
    )(x, g.reshape(1, d))


def _mm_kernel(*refs, nk, epilogue):
    if epilogue == "resid":
        a_ref, b_ref, x_ref, gate_ref, o_ref = refs[:5]
        scratch = refs[5:]
    else:
        a_ref, b_ref, o_ref = refs[:3]
        scratch = refs[3:]

    def finish(acc):
        if epilogue == "relu2":
            r = jnp.maximum(acc, 0.0)
            o_ref[...] = (r * r).astype(o_ref.dtype)
        elif epilogue == "resid":
            o_ref[...] = x_ref[...] + gate_ref[...] * acc
        else:
            o_ref[...] = acc.astype(o_ref.dtype)

    part = jnp.dot(a_ref[...], b_ref[...], preferred_element_type=F32)
    if nk == 1:
        finish(part)
    else:
        acc_ref = scratch[0]
        k = pl.program_id(2)

        @pl.when(k == 0)
        def _():
            acc_ref[...] = part

        @pl.when(k > 0)
        def _():
            acc_ref[...] += part

        @pl.when(k == nk - 1)
        def _():
            finish(acc_ref[...])


def _matmul(a, b, *, out_dtype, epilogue="plain", resid=None, tm=1024, tn=1024, tk=4096):
    m, kdim = a.shape
    n = b.shape[1]
    tm, tn, tk = _pick(m, tm), _pick(n, tn), _pick(kdim, tk)
    if resid is not None:
        tm = _pick(math.gcd(resid[2].n_p, resid[2].t_l), tm)
    nk = kdim // tk
    in_specs = [pl.BlockSpec((tm, tk), lambda i, j, k: (i, k)),
                pl.BlockSpec((tk, tn), lambda i, j, k: (k, j))]
    args = [a, b]
    if epilogue == "resid":
        x, mods3, rows, layer, which = resid
        in_specs += [pl.BlockSpec((tm, tn), lambda i, j, k: (i, j)),
                     pl.BlockSpec((None, 1, tn),
                                  lambda i, j, k: (rows.mod_row(layer, which, i * tm), 0, j))]
        args += [x, mods3]
    out_bytes = jnp.dtype(out_dtype).itemsize
    vmem = 2 * (tm * tk + tk * tn) * 2 + 2 * tm * tn * out_bytes
    if epilogue == "resid":
        vmem += 2 * tm * tn * 4
    if nk > 1:
        vmem += tm * tn * 4
    vmem += 2 * tm * tn * 4
    return pl.pallas_call(
        functools.partial(_mm_kernel, nk=nk, epilogue=epilogue),
        grid=(m // tm, n // tn, nk),
        in_specs=in_specs,
        out_specs=pl.BlockSpec((tm, tn), lambda i, j, k: (i, j)),
        out_shape=jax.ShapeDtypeStruct((m, n), out_dtype),
        scratch_shapes=[pltpu.VMEM((tm, tn), F32)] if nk > 1 else [],
        compiler_params=_cparams(("parallel", "parallel", "arbitrary"),
                                 min(60, vmem / 2 ** 20 + 4)),
        name="matmul_" + epilogue,
    )(*args)


def _gla_tables(c):
    nlev = c.bit_length() - 1
    t = np.arange(c)[:, None]
    r = np.arange(c)[None, :]
    a_f = [(r <= t)]
    a_b = [(r >= t)]
    for lev in range(1, nlev + 1):
        p, hh = 1 << lev, 1 << (lev - 1)
        mid = (t // p) * p + hh
        left = (t % p) < hh
        a_f.append(np.where(left, (r > t) & (r <= mid - 1), (r >= mid) & (r <= t)))
        a_b.append(np.where(left, (r >= t) & (r < mid), (r >= mid) & (r < t)))
    a_all = np.stack([np.concatenate(a_f, 0), np.concatenate(a_b, 0)]).astype(np.float32)
    x = np.bitwise_xor(t, r)
    lvl = np.where(x == 0, 0, np.floor(np.log2(np.maximum(x, 1))).astype(np.int64) + 1)
    lv = np.stack([np.where(r <= t, lvl, -1), np.where(r >= t, lvl, -1)]).astype(np.int32)
    return jnp.asarray(a_all, BF16), jnp.asarray(lv)


def _split3(x):
    x1 = x.astype(BF16)
    r1 = x - x1.astype(F32)
    x2 = r1.astype(BF16)
    x3 = (r1 - x2.astype(F32)).astype(BF16)
    return x1, x2, x3


def _dot_nt(a, b):
    return lax.dot_general(a, b, (((1,), (1,)), ((), ())), preferred_element_type=F32)


def _gla_direction(q, k, v, a1, wa2, ba, a_all, lv, s_ref, *, c, hk, edge_row):
    nlev = c.bit_length() - 1
    a_hi = a1.astype(BF16)
    a_lo = (a1 - a_hi.astype(F32)).astype(BF16)
    w_hi = wa2.astype(BF16)
    w_lo = (wa2 - w_hi.astype(F32)).astype(BF16)
    zz = (jnp.dot(a_hi, w_hi, preferred_element_type=F32)
          + jnp.dot(a_lo, w_hi, preferred_element_type=F32)
          + jnp.dot(a_hi, w_lo, preferred_element_type=F32)) + ba
    g = (jnp.minimum(zz, 0.0) - jnp.log1p(jnp.exp(-jnp.abs(zz)))) * (1.0 / GATE_NORM)
    gs = jnp.concatenate(_split3(g), axis=1)
    ee = jnp.dot(a_all, gs, preferred_element_type=F32)
    ee = ee[:, :hk] + ee[:, hk:2 * hk] + ee[:, 2 * hk:]
    b = ee[:c]
    qs = q * (hk ** -0.5)
    kb = k.astype(BF16)
    vb = v.astype(BF16)
    s = s_ref[...]
    o = jnp.dot((qs * jnp.exp(b)).astype(BF16), s.astype(BF16), preferred_element_type=F32)
    att = jnp.where(lv == 0, _dot_nt(qs.astype(BF16), kb), 0.0)
    for lev in range(1, nlev + 1):
        u = jnp.exp(ee[lev * c:(lev + 1) * c])
        att = att + jnp.where(lv == lev,
                              _dot_nt((qs * u).astype(BF16), (k * u).astype(BF16)), 0.0)
    o = o + jnp.dot(att.astype(BF16), vb, preferred_element_type=F32)
    edge = b[edge_row:edge_row + 1]
    kt = jnp.transpose(k * jnp.exp(edge - b)).astype(BF16)
    dcol = jnp.transpose(jnp.broadcast_to(jnp.exp(edge), (128, hk)))[:, :1]
    s_ref[...] = dcol * s + jnp.dot(kt, vb, preferred_element_type=F32)
    return o


def _gla_kernel(*refs, c, hk, has_init, emit_final):
    (qf, kf, vf, af, qb, kb, vb, ab, wf, bf, wb, bb, aall, lv) = refs[:14]
    pos = 14
    if has_init:
        s0f, s0b = refs[pos:pos + 2]
        pos += 2
    of, ob = refs[pos:pos + 2]
    pos += 2
    if emit_final:
        sff, sfb = refs[pos:pos + 2]
        pos += 2
    sf, sb = refs[pos:pos + 2]
    step = pl.program_id(2)

    @pl.when(step == 0)
    def _():
        if has_init:
            sf[...] = s0f[...]
            sb[...] = s0b[...]
        else:
            sf[...] = jnp.zeros_like(sf)
            sb[...] = jnp.zeros_like(sb)

    of[...] = _gla_direction(qf[...], kf[...], vf[...], af[...], wf[...], bf[...],
                             aall[0], lv[0], sf, c=c, hk=hk, edge_row=c - 1)
    ob[...] = _gla_direction(qb[...], kb[...], vb[...], ab[...], wb[...], bb[...],
                             aall[1], lv[1], sb, c=c, hk=hk, edge_row=0)

    if emit_final:
        @pl.when(step == pl.num_programs(2) - 1)
        def _():
            sff[...] = sf[...]
            sfb[...] = sb[...]


def _gla_scan(z, a1, wa2p_f, ba_f, wa2p_b, ba_b, tables, *, row0, nseq, t, hk, hv,
              init=None, emit_final=False, prev=None):
    m = z.shape[0]
    h = GLA_HEADS
    c = GLA_CHUNK if t % GLA_CHUNK == 0 else t
    nchunk = t // c
    rb0 = row0 // c
    a_all, lv = tables
    fwd = lambda b_, c_: rb0 + b_ * nchunk + c_
    bwd = lambda b_, c_: rb0 + b_ * nchunk + (nchunk - 1 - c_)
    v_col0 = (2 * h * hk) // hv

    def stream(rowfn):
        return [pl.BlockSpec((c, hk), lambda b_, h_, c_: (rowfn(b_, c_), h_)),
                pl.BlockSpec((c, hk), lambda b_, h_, c_: (rowfn(b_, c_), h + h_)),
                pl.BlockSpec((c, hv), lambda b_, h_, c_: (rowfn(b_, c_), v_col0 + h_)),
                pl.BlockSpec((c, A1_WIDTH), lambda b_, h_, c_: (rowfn(b_, c_), 0))]

    head_w = pl.BlockSpec((A1_WIDTH, hk), lambda b_, h_, c_: (0, h_))
    head_b = pl.BlockSpec((1, hk), lambda b_, h_, c_: (0, h_))
    in_specs = (stream(fwd) + stream(bwd) + [head_w, head_b, head_w, head_b]
                + [pl.BlockSpec(a_all.shape, lambda b_, h_, c_: (0, 0, 0)),
                   pl.BlockSpec(lv.shape, lambda b_, h_, c_: (0, 0, 0))])
    args = [z, z, z, a1, z, z, z, a1, wa2p_f, ba_f, wa2p_b, ba_b, a_all, lv]
    state_spec = pl.BlockSpec((None, None, hk, hv), lambda b_, h_, c_: (b_, h_, 0, 0))
    if init is not None:
        in_specs += [state_spec, state_spec]
        args += list(init)
    out_specs = [pl.BlockSpec((c, hv), lambda b_, h_, c_: (fwd(b_, c_), h_)),
                 pl.BlockSpec((c, hv), lambda b_, h_, c_: (bwd(b_, c_), h_))]
    out_shape = [jax.ShapeDtypeStruct((m, h * hv), F32)] * 2
    if emit_final:
        out_specs += [state_spec, state_spec]
        out_shape += [jax.ShapeDtypeStruct((nseq, h, hk, hv), F32)] * 2
    aliases = {}
    n_blocked = len(args)
    if prev is not None:
        aliases = {n_blocked: 0, n_blocked + 1: 1}
        in_specs += [pl.BlockSpec(memory_space=pl.ANY)] * 2
        args += list(prev)

    def body(*refs):
        refs = refs[:n_blocked] + refs[len(args):]
        _gla_kernel(*refs, c=c, hk=hk, has_init=init is not None, emit_final=emit_final)

    return pl.pallas_call(
        body,
        grid=(nseq, h, nchunk),
        in_specs=in_specs,
        out_specs=out_specs,
        out_shape=out_shape,
        scratch_shapes=[pltpu.VMEM((hk, hv), F32)] * 2,
        input_output_aliases=aliases,
        compiler_params=_cparams(("parallel", "parallel", "arbitrary"), 48),
        name="gla_scan",
    )(*args)


def _gla_out_kernel(of_ref, ob_ref, r_ref, g_ref, o_ref, *, hv):
    g = g_ref[...]
    for hd in range(GLA_HEADS):
        sl = slice(hd * hv, (hd + 1) * hv)
        o = of_ref[:, sl] + ob_ref[:, sl]
        o = o * lax.rsqrt(jnp.mean(o * o, axis=-1, keepdims=True) + EPS) * g
        r = r_ref[:, sl]
        o_ref[:, sl] = (o * (r * jax.nn.sigmoid(r))).astype(o_ref.dtype)


def _gla_out(o_f, o_b, z, on_g, hv):
    m, d = o_f.shape
    tm = _pick(m, 256)
    r_col = (z.shape[1] - d) // d
    return pl.pallas_call(
        functools.partial(_gla_out_kernel, hv=hv),
        grid=(m // tm,),
        in_specs=[pl.BlockSpec((tm, d), lambda i: (i, 0)),
                  pl.BlockSpec((tm, d), lambda i: (i, 0)),
                  pl.BlockSpec((tm, d), lambda i: (i, r_col)),
                  pl.BlockSpec((1, hv), lambda i: (0, 0))],
        out_specs=pl.BlockSpec((tm, d), lambda i: (i, 0)),
        out_shape=jax.ShapeDtypeStruct((m, d), BF16),
        compiler_params=_cparams(("parallel",), 48),
        name="gla_out",
    )(o_f, o_b, z, on_g.reshape(1, hv))


def _dft_tables(n):
    idx = jnp.arange(n, dtype=jnp.int32)
    ang = ((idx[:, None] * idx[None, :]) % n).astype(F32) * (2.0 * math.pi / n)
    return jnp.stack([jnp.cos(ang), jnp.sin(ang)]).astype(BF16)


def _chan_dft_kernel(a_ref, w_ref, o_ref):
    o_ref[...] = jnp.dot(a_ref[...], w_ref[...], preferred_element_type=F32).astype(o_ref.dtype)


def _chan_dft(hb, wc):
    m, d = hb.shape
    cg = wc.shape[1]
    tm = _pick(m, 1024)
    return pl.pallas_call(
        _chan_dft_kernel,
        grid=(m // tm, d // cg, 2),
        in_specs=[pl.BlockSpec((tm, cg), lambda i, g, p: (i, g)),
                  pl.BlockSpec((None, cg, cg), lambda i, g, p: (p, 0, 0))],
        out_specs=pl.BlockSpec((None, tm, cg), lambda i, g, p: (p, i, g)),
        out_shape=jax.ShapeDtypeStruct((2, m, d), BF16),
        compiler_params=_cparams(("parallel", "parallel", "arbitrary"), 40),
        name="chan_dft",
    )(hb, wc)


def _seq_dft_kernel(w_ref, p_ref, o_ref, acc_ref, *, nk, kt, scale):
    k = pl.program_id(3)
    part = jnp.dot(w_ref[...], p_ref[...], preferred_element_type=F32)
    sign = jnp.where(k < kt, 1.0, -1.0)

    @pl.when(k == 0)
    def _():
        acc_ref[...] = part

    @pl.when(k > 0)
    def _():
        acc_ref[...] += sign * part

    @pl.when(k == nk - 1)
    def _():
        o_ref[...] = (acc_ref[...] * scale).astype(o_ref.dtype)


def _seq_dft(planes, wt, y_prev, *, row0, nseq, t, scale):
    _, m, d = planes.shape
    tm, tn, tk = _pick(t, 1024), _pick(d, 1024), _pick(t, 2048)
    kt = t // tk
    nk = 2 * kt
    mt = t // tm
    in_specs = [pl.BlockSpec((None, tm, tk), lambda b, i, j, k: (k // kt, i, k % kt)),
                pl.BlockSpec((None, tk, tn),
                             lambda b, i, j, k: (k // kt, (row0 + b * t) // tk + k % kt, j))]
    args = [wt, planes]
    aliases = {}
    if y_prev is not None:
        in_specs.append(pl.BlockSpec(memory_space=pl.ANY))
        args.append(y_prev)
        aliases = {2: 0}

    def body(w_ref, p_ref, *rest):
        o_ref, acc_ref = rest[-2:]
        _seq_dft_kernel(w_ref, p_ref, o_ref, acc_ref, nk=nk, kt=kt, scale=scale)

    return pl.pallas_call(
        body,
        grid=(nseq, mt, d // tn, nk),
        in_specs=in_specs,
        out_specs=pl.BlockSpec((tm, tn), lambda b, i, j, k: ((row0 + b * t) // tm + i, j)),
        out_shape=jax.ShapeDtypeStruct((m, d), BF16),
        scratch_shapes=[pltpu.VMEM((tm, tn), F32)],
        input_output_aliases=aliases,
        compiler_params=_cparams(("parallel", "parallel", "parallel", "arbitrary"), 40),
        name="seq_dft",
    )(*args)


def _pos_embed_2d(n_tok, d):
    rows = n_tok // GRID_W
    quarter = d // 4
    omega = 1.0 / (10000.0 ** (jnp.arange(quarter, dtype=F32) / quarter))
    er = jnp.arange(rows, dtype=F32)[:, None] * omega
    ec = jnp.arange(GRID_W, dtype=F32)[:, None] * omega
    er = jnp.concatenate([jnp.sin(er), jnp.cos(er)], axis=-1)
    ec = jnp.concatenate([jnp.sin(ec), jnp.cos(ec)], axis=-1)
    half = d // 2
    emb = jnp.concatenate([jnp.broadcast_to(er[:, None, :], (rows, GRID_W, half)),
                           jnp.broadcast_to(ec[None, :, :], (rows, GRID_W, half))], axis=-1)
    return emb.reshape(rows * GRID_W, d)


def kernel(x_prompt, x_sample, state_fwd, state_bwd, c, c_ctx, norm1_g, norm2_g, w_mod, b_mod,
           gla_w_in, gla_wa1_f, gla_wa2_f, gla_ba_f, gla_wa1_b, gla_wa2_b, gla_ba_b,
           gla_onorm_g, gla_w_out, fnet_w, mlp_w1, mlp_w2, final_g):
    batch, seq, d = x_prompt.shape
    dec_batch, dec_seq, _ = x_sample.shape
    depth = w_mod.shape[0]
    n_p, n_l = batch * seq, dec_batch * dec_seq
    h = GLA_HEADS
    hk, hv = gla_wa2_f.shape[2] // h, gla_onorm_g.shape[1]
    rank = gla_wa1_f.shape[2]
    cg = d // F_GROUPS

    n_cond = -(-(1 + dec_batch) // 8) * 8
    cond = jnp.zeros((n_cond, d), F32).at[0].set(c_ctx).at[1:1 + dec_batch].set(c)
    mods = _modulation(cond, w_mod, b_mod)
    mods3 = mods.reshape(depth * n_cond * N_MOD, 1, d)
    rows = _Rows(n_p, dec_seq, n_cond)

    x = _embed(x_prompt.reshape(n_p, d), x_sample.reshape(n_l, d), _pos_embed_2d(dec_seq, d))

    tables = _gla_tables(GLA_CHUNK)
    dft_c = _dft_tables(cg)
    dft_tp = _dft_tables(seq)
    dft_tl = _dft_tables(dec_seq)
    new_f, new_b = [], []
    for i in range(depth):
        j = i // 2
        hb = _norm_mod(x, norm1_g[i], mods3, rows, i, 0)
        resid = lambda which: (x, mods3, rows, i, which)
        if i % 2 == 0:
            z = _matmul(hb, gla_w_in[j].astype(BF16), out_dtype=F32)
            wa1 = jnp.zeros((d, A1_WIDTH), F32)
            wa1 = wa1.at[:, :rank].set(gla_wa1_f[j]).at[:, rank:2 * rank].set(gla_wa1_b[j])
            a1 = _matmul(hb, wa1.astype(BF16), out_dtype=F32, tm=2048, tn=A1_WIDTH)
            wa2p_f = jnp.zeros((A1_WIDTH, h * hk), F32).at[:rank].set(gla_wa2_f[j])
            wa2p_b = jnp.zeros((A1_WIDTH, h * hk), F32).at[rank:2 * rank].set(gla_wa2_b[j])
            gate_args = (wa2p_f, gla_ba_f[j].reshape(1, -1), wa2p_b, gla_ba_b[j].reshape(1, -1))
            o_f, o_b, s_f, s_b = _gla_scan(z, a1, *gate_args, tables, row0=0, nseq=batch, t=seq,
                                           hk=hk, hv=hv, emit_final=True)
            o_f, o_b = _gla_scan(z, a1, *gate_args, tables, row0=n_p, nseq=dec_batch,
                                 t=dec_seq, hk=hk, hv=hv, prev=(o_f, o_b),
                                 init=(state_fwd[:, j], state_bwd[:, j]))
            new_f.append(s_f)
            new_b.append(s_b)
            y = _gla_out(o_f, o_b, z, gla_onorm_g[j], hv)
            x = _matmul(y, gla_w_out[j].astype(BF16), out_dtype=F32, epilogue="resid",
                        resid=resid(2), tk=2048)
        else:
            planes = _chan_dft(hb, dft_c)
            y = _seq_dft(planes, dft_tp, None, row0=0, nseq=batch, t=seq,
                         scale=1.0 / math.sqrt(seq * cg))
            y = _seq_dft(planes, dft_tl, y, row0=n_p, nseq=dec_batch, t=dec_seq,
                         scale=1.0 / math.sqrt(dec_seq * cg))
            x = _matmul(y, fnet_w[j].astype(BF16), out_dtype=F32, epilogue="resid",
                        resid=resid(2), tk=2048)
        hb = _norm_mod(x, norm2_g[i], mods3, rows, i, 3)
        a = _matmul(hb, mlp_w1[i].astype(BF16), out_dtype=BF16, epilogue="relu2")
        x = _matmul(a, mlp_w2[i].astype(BF16), out_dtype=F32, epilogue="resid", resid=resid(5),
                    tk=2048)
    y_prompt = _final_norm(x, final_g, 0, n_p).reshape(batch, seq, d)
    y_sample = _final_norm(x, final_g, n_p, n_l).reshape(dec_batch, dec_seq, d)
    return (y_prompt, y_sample, jnp.stack(new_f, axis=1), jnp.stack(new_b, axis=1))
```

```python
import functools
import math

import numpy as np
import jax
import jax.numpy as jnp
from jax import lax
from jax.experimental import pallas as pl
from jax.experimental.pallas import tpu as pltpu

F32 = jnp.float32
BF16 = jnp.bfloat16

EPS = 1e-6
GLA_HEADS = 8
GATE_NORM = 16.0
F_GROUPS = 4
GRID_W = 64
N_MOD = 6
GLA_CHUNK = 128
GLA_HEADS_PER_STEP = 4
GLA_TABLE_LEVELS = 3
A1_WIDTH = 128


def _pick(dim, pref):
    t = min(dim, pref)
    while dim % t:
        t //= 2
    return t


def _cparams(semantics, vmem_mib):
    return pltpu.CompilerParams(dimension_semantics=semantics,
                                vmem_limit_bytes=int(vmem_mib) << 20)


class _Rows:
    def __init__(self, n_p, t_l, n_cond):
        self.n_p, self.t_l, self.n_cond = n_p, t_l, n_cond

    def cond(self, row0):
        return jnp.where(row0 < self.n_p, 0, 1 + (row0 - self.n_p) // self.t_l)

    def mod_row(self, layer, which, row0):
        return (layer * self.n_cond + self.cond(row0)) * N_MOD + which


def _mod_kernel(c_ref, w_ref, b_ref, o_ref):
    c = c_ref[...]
    a = (c * jax.nn.sigmoid(c)).astype(BF16)
    o_ref[...] = jnp.dot(a, w_ref[...].astype(BF16), preferred_element_type=F32) + b_ref[...]


def _modulation(cond, w_mod, b_mod):
    depth, d, n = w_mod.shape
    r = cond.shape[0]
    tn = _pick(n, 512)
    return pl.pallas_call(
        _mod_kernel,
        grid=(depth, n // tn),
        in_specs=[pl.BlockSpec((r, d), lambda l, j: (0, 0)),
                  pl.BlockSpec((None, d, tn), lambda l, j: (l, 0, j)),
                  pl.BlockSpec((None, 1, tn), lambda l, j: (l, 0, j))],
        out_specs=pl.BlockSpec((None, r, tn), lambda l, j: (l, 0, j)),
        out_shape=jax.ShapeDtypeStruct((depth, r, n), F32),
        compiler_params=_cparams(("parallel", "parallel"), 40),
        name="modulation",
    )(cond, w_mod, b_mod.reshape(depth, 1, n))


def _embed_kernel(xp_ref, xl_ref, pos_ref, o_ref, *, n_pblk):
    i = pl.program_id(0)

    @pl.when(i < n_pblk)
    def _():
        o_ref[...] = xp_ref[...]

    @pl.when(i >= n_pblk)
    def _():
        o_ref[...] = xl_ref[...] + pos_ref[...]


def _embed(xp2, xl2, pos):
    n_p, d = xp2.shape
    n_l = xl2.shape[0]
    t_l = pos.shape[0]
    tm = _pick(math.gcd(n_p, t_l), 256)
    n_pblk, n_posblk = n_p // tm, t_l // tm
    return pl.pallas_call(
        functools.partial(_embed_kernel, n_pblk=n_pblk),
        grid=((n_p + n_l) // tm,),
        in_specs=[pl.BlockSpec((tm, d), lambda i: (jnp.minimum(i, n_pblk - 1), 0)),
                  pl.BlockSpec((tm, d), lambda i: (jnp.maximum(i - n_pblk, 0), 0)),
                  pl.BlockSpec((tm, d), lambda i: (jnp.maximum(i - n_pblk, 0) % n_posblk, 0))],
        out_specs=pl.BlockSpec((tm, d), lambda i: (i, 0)),
        out_shape=jax.ShapeDtypeStruct((n_p + n_l, d), F32),
        compiler_params=_cparams(("parallel",), 48),
        name="embed",
    )(xp2, xl2, pos)


def _norm_mod_kernel(x_ref, g_ref, scale_ref, shift_ref, o_ref):
    x = x_ref[...]
    y = x * lax.rsqrt(jnp.mean(x * x, axis=-1, keepdims=True) + EPS)
    o_ref[...] = ((y * g_ref[...]) * (1.0 + scale_ref[...]) + shift_ref[...]).astype(o_ref.dtype)


def _norm_mod(x, g, mods3, rows, layer, which_shift):
    m, d = x.shape
    tm = _pick(math.gcd(rows.n_p, rows.t_l), 512)
    mod_spec = lambda which: pl.BlockSpec(
        (None, 1, d), lambda i: (rows.mod_row(layer, which, i * tm), 0, 0))
    return pl.pallas_call(
        _norm_mod_kernel,
        grid=(m // tm,),
        in_specs=[pl.BlockSpec((tm, d), lambda i: (i, 0)),
                  pl.BlockSpec((1, d), lambda i: (0, 0)),
                  mod_spec(which_shift + 1), mod_spec(which_shift)],
        out_specs=pl.BlockSpec((tm, d), lambda i: (i, 0)),
        out_shape=jax.ShapeDtypeStruct((m, d), BF16),
        compiler_params=_cparams(("parallel",), 48),
        name="norm_mod",
    )(x, g.reshape(1, d), mods3, mods3)


def _final_norm_kernel(x_ref, g_ref, o_ref):
    x = x_ref[...]
    y = x * lax.rsqrt(jnp.mean(x * x, axis=-1, keepdims=True) + EPS)
    o_ref[...] = y * g_ref[...]


def _final_norm(x, g, row0, nrows):
    d = x.shape[1]
    tm = _pick(math.gcd(row0, nrows) if row0 else nrows, 512)
    blk0 = row0 // tm
    return pl.pallas_call(
        _final_norm_kernel,
        grid=(nrows // tm,),
        in_specs=[pl.BlockSpec((tm, d), lambda i: (blk0 + i, 0)),
                  pl.BlockSpec((1, d), lambda i: (0, 0))],
        out_specs=pl.BlockSpec((tm, d), lambda i: (i, 0)),
        out_shape=jax.ShapeDtypeStruct((nrows, d), F32),
        compiler_params=_cparams(("parallel",), 48),
        name="final_norm",
    )(x, g.reshape(1, d))


def _mm_kernel(*refs, nk, epilogue):
    if epilogue == "resid":
        a_ref, b_ref, x_ref, gate_ref, o_ref = refs[:5]
        scratch = refs[5:]
    else:
        a_ref, b_ref, o_ref = refs[:3]
        scratch = refs[3:]

    def finish(acc):
        if epilogue == "relu2":
            r = jnp.maximum(acc, 0.0)
            o_ref[...] = (r * r).astype(o_ref.dtype)
        elif epilogue == "resid":
            o_ref[...] = x_ref[...] + gate_ref[...] * acc
        else:
            o_ref[...] = acc.astype(o_ref.dtype)

    part = jnp.dot(a_ref[...], b_ref[...], preferred_element_type=F32)
    if nk == 1:
        finish(part)
    elif epilogue == "resid":
        k = pl.program_id(2)

        @pl.when(k == 0)
        def _():
            o_ref[...] = part

        @pl.when((k > 0) & (k < nk - 1))
        def _():
            o_ref[...] += part

        @pl.when(k == nk - 1)
        def _():
            finish(o_ref[...] + part)
    else:
        acc_ref = scratch[0]
        k = pl.program_id(2)

        @pl.when(k == 0)
        def _():
            acc_ref[...] = part

        @pl.when(k > 0)
        def _():
            acc_ref[...] += part

        @pl.when(k == nk - 1)
        def _():
            finish(acc_ref[...])


def _matmul(a, b, *, out_dtype, epilogue="plain", resid=None, w_layer=0,
            tm=1024, tn=1024, tk=4096):
    m, kdim = a.shape
    n = b.shape[2]
    tm, tn, tk = _pick(m, tm), _pick(n, tn), _pick(kdim, tk)
    if resid is not None:
        tm = _pick(math.gcd(resid[2].n_p, resid[2].t_l), tm)
    nk = kdim // tk
    in_specs = [pl.BlockSpec((tm, tk), lambda i, j, k: (i, k)),
                pl.BlockSpec((None, tk, tn), lambda i, j, k: (w_layer, k, j))]
    args = [a, b]
    if epilogue == "resid":
        x, mods3, rows, layer, which = resid
        in_specs += [pl.BlockSpec((tm, tn), lambda i, j, k: (i, j)),
                     pl.BlockSpec((None, 1, tn),
                                  lambda i, j, k: (rows.mod_row(layer, which, i * tm), 0, j))]
        args += [x, mods3]
    out_bytes = jnp.dtype(out_dtype).itemsize
    vmem = 2 * (tm * tk + tk * tn) * 2 + 2 * tm * tn * out_bytes
    if epilogue == "resid":
        vmem += 2 * tm * tn * 4
    use_acc = nk > 1 and epilogue != "resid"
    if use_acc:
        vmem += tm * tn * 4
    vmem += 2 * tm * tn * 4
    return pl.pallas_call(
        functools.partial(_mm_kernel, nk=nk, epilogue=epilogue),
        grid=(m // tm, n // tn, nk),
        in_specs=in_specs,
        out_specs=pl.BlockSpec((tm, tn), lambda i, j, k: (i, j)),
        out_shape=jax.ShapeDtypeStruct((m, n), out_dtype),
        scratch_shapes=[pltpu.VMEM((tm, tn), F32)] if use_acc else [],
        compiler_params=_cparams(("parallel", "parallel", "arbitrary"),
                                 min(60, vmem / 2 ** 20 + 4)),
        name="matmul_" + epilogue,
    )(*args)


def _gla_tables(c):
    t = np.arange(c)[:, None]
    r = np.arange(c)[None, :]
    a_f = [(r <= t)]
    a_b = [(r >= t)]
    for lev in range(1, GLA_TABLE_LEVELS + 1):
        p, hh = 1 << lev, 1 << (lev - 1)
        mid = (t // p) * p + hh
        left = (t % p) < hh
        a_f.append(np.where(left, (r > t) & (r <= mid - 1), (r >= mid) & (r <= t)))
        a_b.append(np.where(left, (r >= t) & (r < mid), (r >= mid) & (r < t)))
    a_all = np.stack([np.concatenate(a_f, 0), np.concatenate(a_b, 0)]).astype(np.float32)
    a_all = np.tile(a_all, (1, 1, 2))
    x = np.bitwise_xor(t, r)
    lvl = np.where(x == 0, 0, np.floor(np.log2(np.maximum(x, 1))).astype(np.int64) + 1)
    lv = np.stack([np.where(r <= t, lvl, -1), np.where(r >= t, lvl, -1)]).astype(np.int32)
    return jnp.asarray(a_all, BF16), jnp.asarray(lv)


def _split2(x):
    hi = x.astype(BF16)
    return hi, (x - hi.astype(F32)).astype(BF16)


def _gate_kernel(h_ref, wa1_ref, w2hi_ref, w2lo_ref, ba_ref, g1_ref, g2_ref):
    a_hi, a_lo = _split2(jnp.dot(h_ref[...], wa1_ref[...], preferred_element_type=F32))
    zz = (jnp.dot(a_hi, w2hi_ref[...], preferred_element_type=F32)
          + jnp.dot(a_lo, w2hi_ref[...], preferred_element_type=F32)
          + jnp.dot(a_hi, w2lo_ref[...], preferred_element_type=F32)) + ba_ref[...]
    g = (jnp.minimum(zz, 0.0) - jnp.log(1.0 + jnp.exp(-jnp.abs(zz)))) * (1.0 / GATE_NORM)
    g1_ref[...], g2_ref[...] = _split2(g)


def _log_gates(hb, wa1, w_layer, wa2, ba):
    m, d = hb.shape
    r, n = wa2.shape
    tm = _pick(m, 256)
    w2hi, w2lo = _split2(wa2)
    full = lambda shape: pl.BlockSpec(shape, lambda i: (0,) * len(shape))
    return pl.pallas_call(
        _gate_kernel,
        grid=(m // tm,),
        in_specs=[pl.BlockSpec((tm, d), lambda i: (i, 0)),
                  pl.BlockSpec((None, d, r), lambda i: (w_layer, 0, 0)),
                  full((r, n)), full((r, n)), full((1, n))],
        out_specs=[pl.BlockSpec((tm, n), lambda i: (i, 0))] * 2,
        out_shape=[jax.ShapeDtypeStruct((m, n), BF16)] * 2,
        compiler_params=_cparams(("parallel",), 48),
        name="log_gates",
    )(hb, wa1, w2hi, w2lo, ba)


def _dot_nt(a, b):
    return lax.dot_general(a, b, (((1,), (1,)), ((), ())), preferred_element_type=F32)


def _gla_chunks(chains, a_all_ref, lv_ref, *, c, hk):
    nlev = c.bit_length() - 1
    n = range(len(chains))
    dirs = [ch[0] for ch in chains]
    edge_row = {0: c - 1, 1: 0}
    ee = [jnp.dot(a_all_ref[dirs[i]], jnp.concatenate(chains[i][4](), axis=0),
                  preferred_element_type=F32) for i in n]
    q = lambda i: chains[i][1]()
    k = lambda i: chains[i][2]()
    scale = hk ** -0.5
    att = [jnp.where(lv_ref[dirs[i]] == 0, _dot_nt(q(i).astype(BF16), k(i).astype(BF16)), 0.0)
           for i in n]
    def level_exponent(i, lev):
        if lev <= GLA_TABLE_LEVELS:
            return ee[i][lev * c:(lev + 1) * c]
        b = ee[i][:c]
        p, ref0 = 1 << lev, (1 << (lev - 1)) - 1 + dirs[i]
        ref = [jnp.broadcast_to(b[p0 + ref0:p0 + ref0 + 1], (p, hk)) for p0 in range(0, c, p)]
        return -jnp.abs(b - (ref[0] if len(ref) == 1 else jnp.concatenate(ref, axis=0)))

    for lev in range(1, nlev + 1):
        for i in n:
            u = jnp.exp(level_exponent(i, lev))
            att[i] = att[i] + jnp.where(
                lv_ref[dirs[i]] == lev,
                _dot_nt((q(i) * u).astype(BF16), (k(i) * u).astype(BF16)), 0.0)
    for i in n:
        b = ee[i][:c]
        vb = chains[i][3]().astype(BF16)
        st = chains[i][5][...]
        o = jnp.dot((q(i) * jnp.exp(b)).astype(BF16), st.astype(BF16), preferred_element_type=F32)
        o = o + jnp.dot(att[i].astype(BF16), vb, preferred_element_type=F32)
        chains[i][6](o * scale)
        er = edge_row[dirs[i]]
        edge = b[er:er + 1]
        kt = jnp.transpose(k(i) * jnp.exp(edge - b)).astype(BF16)
        dcol = jnp.transpose(jnp.broadcast_to(jnp.exp(edge), (128, hk)))[:, :1]
        chains[i][5][...] = dcol * st + jnp.dot(kt, vb, preferred_element_type=F32)


def _gla_kernel(*refs, c, hk, hpb, has_init, emit_final):
    (qf, kf, vf, g1f, g2f, qb, kb, vb, g1b, g2b, aall, lv) = refs[:12]
    pos = 12
    if has_init:
        s0f, s0b = refs[pos:pos + 2]
        pos += 2
    of, ob = refs[pos:pos + 2]
    pos += 2
    if emit_final:
        sff, sfb = refs[pos:pos + 2]
        pos += 2
    sf, sb = refs[pos:pos + 2]
    step = pl.program_id(2)

    @pl.when(step == 0)
    def _():
        if has_init:
            sf[...] = s0f[...]
            sb[...] = s0b[...]
        else:
            sf[...] = jnp.zeros_like(sf)
            sb[...] = jnp.zeros_like(sb)

    hv = of.shape[1] // hpb

    chains = []
    for d, (q, k, v, g1, g2, s, o) in enumerate(((qf, kf, vf, g1f, g2f, sf, of),
                                                 (qb, kb, vb, g1b, g2b, sb, ob))):
        for hh in range(hpb):
            ksl = slice(hh * hk, (hh + 1) * hk)
            vsl = slice(hh * hv, (hh + 1) * hv)

            def store(val, o=o, vsl=vsl):
                o[:, vsl] = val

            chains.append((d,
                           lambda q=q, ksl=ksl: q[:, ksl],
                           lambda k=k, ksl=ksl: k[:, ksl],
                           lambda v=v, vsl=vsl: v[:, vsl],
                           lambda g1=g1, g2=g2, ksl=ksl: (g1[:, ksl], g2[:, ksl]),
                           s.at[hh], store))
    _gla_chunks(chains, aall, lv, c=c, hk=hk)

    if emit_final:
        @pl.when(step == pl.num_programs(2) - 1)
        def _():
            sff[...] = sf[...]
            sfb[...] = sb[...]


def _gla_scan(z, g1, g2, tables, *, row0, nseq, t, hk, hv,
              layer, n_layers, init=None, final=None, emit_final=False, prev=None):
    m = z.shape[0]
    h = GLA_HEADS
    hpb = GLA_HEADS_PER_STEP
    c = GLA_CHUNK if t % GLA_CHUNK == 0 else t
    nchunk = t // c
    rb0 = row0 // c
    a_all, lv = tables
    fwd = lambda b_, c_: rb0 + b_ * nchunk + c_
    bwd = lambda b_, c_: rb0 + b_ * nchunk + (nchunk - 1 - c_)
    k_col0 = h // hpb
    v_col0 = (2 * h * hk) // (hpb * hv)

    def stream(rowfn, d):
        gate = pl.BlockSpec((c, hpb * hk), lambda b_, h_, c_: (rowfn(b_, c_), d * k_col0 + h_))
        return [pl.BlockSpec((c, hpb * hk), lambda b_, h_, c_: (rowfn(b_, c_), h_)),
                pl.BlockSpec((c, hpb * hk), lambda b_, h_, c_: (rowfn(b_, c_), k_col0 + h_)),
                pl.BlockSpec((c, hpb * hv), lambda b_, h_, c_: (rowfn(b_, c_), v_col0 + h_)),
                gate, gate]

    in_specs = (stream(fwd, 0) + stream(bwd, 1)
                + [pl.BlockSpec(a_all.shape, lambda b_, h_, c_: (0, 0, 0)),
                   pl.BlockSpec(lv.shape, lambda b_, h_, c_: (0, 0, 0))])
    args = [z, z, z, g1, g2, z, z, z, g1, g2, a_all, lv]
    state_spec = pl.BlockSpec((None, None, hpb, hk, hv),
                              lambda b_, h_, c_: (b_, layer, h_, 0, 0))
    if init is not None:
        in_specs += [state_spec, state_spec]
        args += list(init)
    out_specs = [pl.BlockSpec((c, hpb * hv), lambda b_, h_, c_: (fwd(b_, c_), h_)),
                 pl.BlockSpec((c, hpb * hv), lambda b_, h_, c_: (bwd(b_, c_), h_))]
    out_shape = [jax.ShapeDtypeStruct((m, h * hv), F32)] * 2
    if emit_final:
        out_specs += [state_spec, state_spec]
        out_shape += [jax.ShapeDtypeStruct((nseq, n_layers, h, hk, hv), F32)] * 2
    aliases = {}
    n_blocked = len(args)
    for bufs, out0 in ((prev, 0), (final, 2)):
        if bufs is not None:
            aliases.update({len(args): out0, len(args) + 1: out0 + 1})
            in_specs += [pl.BlockSpec(memory_space=pl.ANY)] * 2
            args += list(bufs)

    def body(*refs):
        refs = refs[:n_blocked] + refs[len(args):]
        _gla_kernel(*refs, c=c, hk=hk, hpb=hpb, has_init=init is not None,
                    emit_final=emit_final)

    return pl.pallas_call(
        body,
        grid=(nseq, h // hpb, nchunk),
        in_specs=in_specs,
        out_specs=out_specs,
        out_shape=out_shape,
        scratch_shapes=[pltpu.VMEM((hpb, hk, hv), F32)] * 2,
        input_output_aliases=aliases,
        compiler_params=_cparams(("parallel", "parallel", "arbitrary"), 56),
        name="gla_scan",
    )(*args)


def _gla_out_kernel(of_ref, ob_ref, r_ref, g_ref, o_ref, *, hv):
    g = g_ref[...]
    for hd in range(GLA_HEADS):
        sl = slice(hd * hv, (hd + 1) * hv)
        o = of_ref[:, sl] + ob_ref[:, sl]
        o = o * lax.rsqrt(jnp.mean(o * o, axis=-1, keepdims=True) + EPS) * g
        r = r_ref[:, sl]
        o_ref[:, sl] = (o * (r * jax.nn.sigmoid(r))).astype(o_ref.dtype)


def _gla_out(o_f, o_b, z, on_g, hv):
    m, d = o_f.shape
    tm = _pick(m, 256)
    r_col = (z.shape[1] - d) // d
    return pl.pallas_call(
        functools.partial(_gla_out_kernel, hv=hv),
        grid=(m // tm,),
        in_specs=[pl.BlockSpec((tm, d), lambda i: (i, 0)),
                  pl.BlockSpec((tm, d), lambda i: (i, 0)),
                  pl.BlockSpec((tm, d), lambda i: (i, r_col)),
                  pl.BlockSpec((1, hv), lambda i: (0, 0))],
        out_specs=pl.BlockSpec((tm, d), lambda i: (i, 0)),
        out_shape=jax.ShapeDtypeStruct((m, d), BF16),
        compiler_params=_cparams(("parallel",), 48),
        name="gla_out",
    )(o_f, o_b, z, on_g.reshape(1, hv))


def _dft_tables(n):
    idx = jnp.arange(n, dtype=jnp.int32)
    ang = ((idx[:, None] * idx[None, :]) % n).astype(F32) * (2.0 * math.pi / n)
    return jnp.stack([jnp.cos(ang), jnp.sin(ang)]).astype(BF16)


def _chan_dft_kernel(a_ref, w_ref, o_ref):
    o_ref[...] = jnp.dot(a_ref[...], w_ref[...], preferred_element_type=F32).astype(o_ref.dtype)


def _chan_dft(hb, wc):
    m, d = hb.shape
    cg = wc.shape[1]
    tm = _pick(m, 1024)
    return pl.pallas_call(
        _chan_dft_kernel,
        grid=(m // tm, d // cg, 2),
        in_specs=[pl.BlockSpec((tm, cg), lambda i, g, p: (i, g)),
                  pl.BlockSpec((None, cg, cg), lambda i, g, p: (p, 0, 0))],
        out_specs=pl.BlockSpec((None, tm, cg), lambda i, g, p: (p, i, g)),
        out_shape=jax.ShapeDtypeStruct((2, m, d), BF16),
        compiler_params=_cparams(("parallel", "parallel", "arbitrary"), 40),
        name="chan_dft",
    )(hb, wc)


def _seq_dft_kernel(w_ref, p_ref, o_ref, acc_ref, *, nk, kt, scale):
    k = pl.program_id(3)
    part = jnp.dot(w_ref[...], p_ref[...], preferred_element_type=F32)
    sign = jnp.where(k < kt, 1.0, -1.0)

    @pl.when(k == 0)
    def _():
        acc_ref[...] = part

    @pl.when(k > 0)
    def _():
        acc_ref[...] += sign * part

    @pl.when(k == nk - 1)
    def _():
        o_ref[...] = (acc_ref[...] * scale).astype(o_ref.dtype)


def _seq_dft(planes, wt, y_prev, *, row0, nseq, t, scale):
    _, m, d = planes.shape
    tm, tn, tk = _pick(t, 1024), _pick(d, 1024), _pick(t, 2048)
    kt = t // tk
    nk = 2 * kt
    mt = t // tm
    in_specs = [pl.BlockSpec((None, tm, tk), lambda b, i, j, k: (k // kt, i, k % kt)),
                pl.BlockSpec((None, tk, tn),
                             lambda b, i, j, k: (k // kt, (row0 + b * t) // tk + k % kt, j))]
    args = [wt, planes]
    aliases = {}
    if y_prev is not None:
        in_specs.append(pl.BlockSpec(memory_space=pl.ANY))
        args.append(y_prev)
        aliases = {2: 0}

    def body(w_ref, p_ref, *rest):
        o_ref, acc_ref = rest[-2:]
        _seq_dft_kernel(w_ref, p_ref, o_ref, acc_ref, nk=nk, kt=kt, scale=scale)

    return pl.pallas_call(
        body,
        grid=(nseq, mt, d // tn, nk),
        in_specs=in_specs,
        out_specs=pl.BlockSpec((tm, tn), lambda b, i, j, k: ((row0 + b * t) // tm + i, j)),
        out_shape=jax.ShapeDtypeStruct((m, d), BF16),
        scratch_shapes=[pltpu.VMEM((tm, tn), F32)],
        input_output_aliases=aliases,
        compiler_params=_cparams(("parallel", "parallel", "parallel", "arbitrary"), 40),
        name="seq_dft",
    )(*args)


def _pos_embed_2d(n_tok, d):
    rows = n_tok // GRID_W
    quarter = d // 4
    omega = 1.0 / (10000.0 ** (jnp.arange(quarter, dtype=F32) / quarter))
    er = jnp.arange(rows, dtype=F32)[:, None] * omega
    ec = jnp.arange(GRID_W, dtype=F32)[:, None] * omega
    er = jnp.concatenate([jnp.sin(er), jnp.cos(er)], axis=-1)
    ec = jnp.concatenate([jnp.sin(ec), jnp.cos(ec)], axis=-1)
    half = d // 2
    emb = jnp.concatenate([jnp.broadcast_to(er[:, None, :], (rows, GRID_W, half)),
                           jnp.broadcast_to(ec[None, :, :], (rows, GRID_W, half))], axis=-1)
    return emb.reshape(rows * GRID_W, d)


def kernel(x_prompt, x_sample, state_fwd, state_bwd, c, c_ctx, norm1_g, norm2_g, w_mod, b_mod,
           gla_w_in, gla_wa1_f, gla_wa2_f, gla_ba_f, gla_wa1_b, gla_wa2_b, gla_ba_b,
           gla_onorm_g, gla_w_out, fnet_w, mlp_w1, mlp_w2, final_g):
    batch, seq, d = x_prompt.shape
    dec_batch, dec_seq, _ = x_sample.shape
    depth = w_mod.shape[0]
    n_p, n_l = batch * seq, dec_batch * dec_seq
    h = GLA_HEADS
    hk, hv = gla_wa2_f.shape[2] // h, gla_onorm_g.shape[1]
    rank = gla_wa1_f.shape[2]
    cg = d // F_GROUPS

    n_cond = -(-(1 + dec_batch) // 8) * 8
    cond = jnp.zeros((n_cond, d), F32).at[0].set(c_ctx).at[1:1 + dec_batch].set(c)
    mods = _modulation(cond, w_mod, b_mod)
    mods3 = mods.reshape(depth * n_cond * N_MOD, 1, d)
    rows = _Rows(n_p, dec_seq, n_cond)

    x = _embed(x_prompt.reshape(n_p, d), x_sample.reshape(n_l, d), _pos_embed_2d(dec_seq, d))

    tables = _gla_tables(GLA_CHUNK)
    dft_c = _dft_tables(cg)
    dft_tp = _dft_tables(seq)
    dft_tl = _dft_tables(dec_seq)
    w_in, w_out, w_fnet = (w.astype(BF16) for w in (gla_w_in, gla_w_out, fnet_w))
    w1, w2 = mlp_w1.astype(BF16), mlp_w2.astype(BF16)
    n_gla = gla_w_in.shape[0]
    wa1 = jnp.zeros((n_gla, d, A1_WIDTH), F32)
    wa1 = wa1.at[:, :, :rank].set(gla_wa1_f).at[:, :, rank:2 * rank].set(gla_wa1_b).astype(BF16)
    new_states = None
    for i in range(depth):
        j = i // 2
        hb = _norm_mod(x, norm1_g[i], mods3, rows, i, 0)
        resid = lambda which: (x, mods3, rows, i, which)
        if i % 2 == 0:
            z = _matmul(hb, w_in, w_layer=j, out_dtype=F32)
            wa2 = jnp.zeros((A1_WIDTH, 2 * h * hk), F32)
            wa2 = wa2.at[:rank, :h * hk].set(gla_wa2_f[j])
            wa2 = wa2.at[rank:2 * rank, h * hk:].set(gla_wa2_b[j])
            ba = jnp.concatenate([gla_ba_f[j], gla_ba_b[j]]).reshape(1, -1)
            g1, g2 = _log_gates(hb, wa1, j, wa2, ba)
            common = dict(hk=hk, hv=hv, layer=j, n_layers=n_gla)
            o_f, o_b, s_f, s_b = _gla_scan(z, g1, g2, tables, row0=0, nseq=batch, t=seq,
                                           emit_final=True, final=new_states, **common)
            new_states = (s_f, s_b)
            o_f, o_b = _gla_scan(z, g1, g2, tables, row0=n_p, nseq=dec_batch,
                                 t=dec_seq, prev=(o_f, o_b), init=(state_fwd, state_bwd),
                                 **common)
            y = _gla_out(o_f, o_b, z, gla_onorm_g[j], hv)
            x = _matmul(y, w_out, w_layer=j, out_dtype=F32, epilogue="resid", resid=resid(2))
        else:
            planes = _chan_dft(hb, dft_c)
            y = _seq_dft(planes, dft_tp, None, row0=0, nseq=batch, t=seq,
                         scale=1.0 / math.sqrt(seq * cg))
            y = _seq_dft(planes, dft_tl, y, row0=n_p, nseq=dec_batch, t=dec_seq,
                         scale=1.0 / math.sqrt(dec_seq * cg))
            x = _matmul(y, w_fnet, w_layer=j, out_dtype=F32, epilogue="resid", resid=resid(2))
        hb = _norm_mod(x, norm2_g[i], mods3, rows, i, 3)
        a = _matmul(hb, w1, w_layer=i, out_dtype=BF16, epilogue="relu2")
        x = _matmul(a, w2, w_layer=i, out_dtype=F32, epilogue="resid", resid=resid(5))
    y_prompt = _final_norm(x, final_g, 0, n_p).reshape(batch, seq, d)
    y_sample = _final_norm(x, final_g, n_p, n_l).reshape(dec_batch, dec_seq, d)
    return (y_prompt, y_sample) + tuple(new_states)
```

```python
import functools
import math

import numpy as np
import jax
import jax.numpy as jnp
from jax import lax
from jax.experimental import pallas as pl
from jax.experimental.pallas import tpu as pltpu

F32 = jnp.float32
BF16 = jnp.bfloat16

EPS = 1e-6
GLA_HEADS = 8
GATE_NORM = 16.0
F_GROUPS = 4
GRID_W = 64
N_MOD = 6
GLA_CHUNK = 128
GLA_HEADS_PER_STEP = 4
GLA_TABLE_LEVELS = 3
A1_WIDTH = 128


def _pick(dim, pref):
    t = min(dim, pref)
    while dim % t:
        t //= 2
    return t


def _cparams(semantics, vmem_mib):
    return pltpu.CompilerParams(dimension_semantics=semantics,
                                vmem_limit_bytes=int(vmem_mib) << 20)


class _Rows:
    def __init__(self, n_p, t_l, n_cond):
        self.n_p, self.t_l, self.n_cond = n_p, t_l, n_cond

    def cond(self, row0):
        return jnp.where(row0 < self.n_p, 0, 1 + (row0 - self.n_p) // self.t_l)

    def mod_row(self, layer, which, row0):
        return (layer * self.n_cond + self.cond(row0)) * N_MOD + which


def _mod_kernel(c_ref, w_ref, b_ref, o_ref):
    c = c_ref[...]
    a = (c * jax.nn.sigmoid(c)).astype(BF16)
    o_ref[...] = jnp.dot(a, w_ref[...].astype(BF16), preferred_element_type=F32) + b_ref[...]


def _modulation(cond, w_mod, b_mod):
    depth, d, n = w_mod.shape
    r = cond.shape[0]
    tn = _pick(n, 512)
    return pl.pallas_call(
        _mod_kernel,
        grid=(depth, n // tn),
        in_specs=[pl.BlockSpec((r, d), lambda l, j: (0, 0)),
                  pl.BlockSpec((None, d, tn), lambda l, j: (l, 0, j)),
                  pl.BlockSpec((None, 1, tn), lambda l, j: (l, 0, j))],
        out_specs=pl.BlockSpec((None, r, tn), lambda l, j: (l, 0, j)),
        out_shape=jax.ShapeDtypeStruct((depth, r, n), F32),
        compiler_params=_cparams(("parallel", "parallel"), 40),
        name="modulation",
    )(cond, w_mod, b_mod.reshape(depth, 1, n))


def _norm_mod_rows(x, g_ref, scale_ref, shift_ref):
    y = x * lax.rsqrt(jnp.mean(x * x, axis=-1, keepdims=True) + EPS)
    return (y * g_ref[...]) * (1.0 + scale_ref[...]) + shift_ref[...]


def _embed_kernel(xp_ref, xl_ref, pos_ref, g_ref, scale_ref, shift_ref, x_ref, h_ref, *, n_pblk):
    i = pl.program_id(0)

    @pl.when(i < n_pblk)
    def _():
        x_ref[...] = xp_ref[...]

    @pl.when(i >= n_pblk)
    def _():
        x_ref[...] = xl_ref[...] + pos_ref[...]

    h_ref[...] = _norm_mod_rows(x_ref[...], g_ref, scale_ref, shift_ref).astype(h_ref.dtype)


def _embed(xp2, xl2, pos, g, mods3, rows):
    n_p, d = xp2.shape
    n_l = xl2.shape[0]
    t_l = pos.shape[0]
    tm = _pick(math.gcd(n_p, t_l), 256)
    n_pblk, n_posblk = n_p // tm, t_l // tm
    mod_spec = lambda which: pl.BlockSpec(
        (None, 1, d), lambda i: (rows.mod_row(0, which, i * tm), 0, 0))
    return pl.pallas_call(
        functools.partial(_embed_kernel, n_pblk=n_pblk),
        grid=((n_p + n_l) // tm,),
        in_specs=[pl.BlockSpec((tm, d), lambda i: (jnp.minimum(i, n_pblk - 1), 0)),
                  pl.BlockSpec((tm, d), lambda i: (jnp.maximum(i - n_pblk, 0), 0)),
                  pl.BlockSpec((tm, d), lambda i: (jnp.maximum(i - n_pblk, 0) % n_posblk, 0)),
                  pl.BlockSpec((1, d), lambda i: (0, 0)),
                  mod_spec(1), mod_spec(0)],
        out_specs=[pl.BlockSpec((tm, d), lambda i: (i, 0))] * 2,
        out_shape=[jax.ShapeDtypeStruct((n_p + n_l, d), F32),
                   jax.ShapeDtypeStruct((n_p + n_l, d), BF16)],
        compiler_params=_cparams(("parallel",), 48),
        name="embed",
    )(xp2, xl2, pos, g.reshape(1, d), mods3, mods3)


def _norm_mod_kernel(x_ref, g_ref, scale_ref, shift_ref, o_ref):
    o_ref[...] = _norm_mod_rows(x_ref[...], g_ref, scale_ref, shift_ref).astype(o_ref.dtype)


def _norm_mod(x, g, mods3, rows, layer, which_shift):
    m, d = x.shape
    tm = _pick(math.gcd(rows.n_p, rows.t_l), 512)
    mod_spec = lambda which: pl.BlockSpec(
        (None, 1, d), lambda i: (rows.mod_row(layer, which, i * tm), 0, 0))
    return pl.pallas_call(
        _norm_mod_kernel,
        grid=(m // tm,),
        in_specs=[pl.BlockSpec((tm, d), lambda i: (i, 0)),
                  pl.BlockSpec((1, d), lambda i: (0, 0)),
                  mod_spec(which_shift + 1), mod_spec(which_shift)],
        out_specs=pl.BlockSpec((tm, d), lambda i: (i, 0)),
        out_shape=jax.ShapeDtypeStruct((m, d), BF16),
        compiler_params=_cparams(("parallel",), 48),
        name="norm_mod",
    )(x, g.reshape(1, d), mods3, mods3)


def _final_norm_kernel(x_ref, g_ref, o_ref):
    x = x_ref[...]
    y = x * lax.rsqrt(jnp.mean(x * x, axis=-1, keepdims=True) + EPS)
    o_ref[...] = y * g_ref[...]


def _final_norm(x, g, row0, nrows):
    d = x.shape[1]
    tm = _pick(math.gcd(row0, nrows) if row0 else nrows, 512)
    blk0 = row0 // tm
    return pl.pallas_call(
        _final_norm_kernel,
        grid=(nrows // tm,),
        in_specs=[pl.BlockSpec((tm, d), lambda i: (blk0 + i, 0)),
                  pl.BlockSpec((1, d), lambda i: (0, 0))],
        out_specs=pl.BlockSpec((tm, d), lambda i: (i, 0)),
        out_shape=jax.ShapeDtypeStruct((nrows, d), F32),
        compiler_params=_cparams(("parallel",), 48),
        name="final_norm",
    )(x, g.reshape(1, d))


def _mm_kernel(*refs, nk, epilogue):
    if epilogue == "resid":
        a_ref, b_ref, x_ref, gate_ref, o_ref = refs[:5]
        scratch = refs[5:]
    else:
        a_ref, b_ref, o_ref = refs[:3]
        scratch = refs[3:]

    def finish(acc):
        if epilogue == "relu2":
            r = jnp.maximum(acc, 0.0)
            o_ref[...] = (r * r).astype(o_ref.dtype)
        elif epilogue == "resid":
            o_ref[...] = x_ref[...] + gate_ref[...] * acc
        else:
            o_ref[...] = acc.astype(o_ref.dtype)

    def product():
        return jnp.dot(a_ref[...], b_ref[...], preferred_element_type=F32)

    if nk == 1:
        finish(product())
        return
    acc_ref = o_ref if epilogue == "resid" else scratch[0]
    k = pl.program_id(2)

    @pl.when(k == 0)
    def _():
        acc_ref[...] = product()

    @pl.when((k > 0) & (k < nk - 1))
    def _():
        acc_ref[...] = acc_ref[...] + product()

    @pl.when(k == nk - 1)
    def _():
        finish(acc_ref[...] + product())


def _matmul(a, b, *, out_dtype, epilogue="plain", resid=None, w_layer=0,
            tm=1024, tn=1024, tk=4096):
    m, kdim = a.shape
    n = b.shape[2]
    tm, tn, tk = _pick(m, tm), _pick(n, tn), _pick(kdim, tk)
    if resid is not None:
        tm = _pick(math.gcd(resid[2].n_p, resid[2].t_l), tm)
    nk = kdim // tk
    in_specs = [pl.BlockSpec((tm, tk), lambda i, j, k: (i, k)),
                pl.BlockSpec((None, tk, tn), lambda i, j, k: (w_layer, k, j))]
    args = [a, b]
    if epilogue == "resid":
        x, mods3, rows, layer, which = resid
        in_specs += [pl.BlockSpec((tm, tn), lambda i, j, k: (i, j)),
                     pl.BlockSpec((None, 1, tn),
                                  lambda i, j, k: (rows.mod_row(layer, which, i * tm), 0, j))]
        args += [x, mods3]
    out_bytes = jnp.dtype(out_dtype).itemsize
    vmem = 2 * (tm * tk + tk * tn) * 2 + 2 * tm * tn * out_bytes
    if epilogue == "resid":
        vmem += 2 * tm * tn * 4
    use_acc = nk > 1 and epilogue != "resid"
    if use_acc:
        vmem += tm * tn * 4
    vmem += 2 * tm * tn * 4
    return pl.pallas_call(
        functools.partial(_mm_kernel, nk=nk, epilogue=epilogue),
        grid=(m // tm, n // tn, nk),
        in_specs=in_specs,
        out_specs=pl.BlockSpec((tm, tn), lambda i, j, k: (i, j)),
        out_shape=jax.ShapeDtypeStruct((m, n), out_dtype),
        scratch_shapes=[pltpu.VMEM((tm, tn), F32)] if use_acc else [],
        compiler_params=_cparams(("parallel", "parallel", "arbitrary"),
                                 min(60, vmem / 2 ** 20 + 4)),
        name="matmul_" + epilogue,
    )(*args)


def _gla_tables(c):
    t = np.arange(c)[:, None]
    r = np.arange(c)[None, :]
    a_f = [(r <= t)]
    a_b = [(r >= t)]
    for lev in range(1, GLA_TABLE_LEVELS + 1):
        p, hh = 1 << lev, 1 << (lev - 1)
        mid = (t // p) * p + hh
        left = (t % p) < hh
        a_f.append(np.where(left, (r > t) & (r <= mid - 1), (r >= mid) & (r <= t)))
        a_b.append(np.where(left, (r >= t) & (r < mid), (r >= mid) & (r < t)))
    a_all = np.stack([np.concatenate(a_f, 0), np.concatenate(a_b, 0)]).astype(np.float32)
    a_all = np.tile(a_all, (1, 1, 2))
    x = np.bitwise_xor(t, r)
    lvl = np.where(x == 0, 0, np.floor(np.log2(np.maximum(x, 1))).astype(np.int64) + 1)
    lv = np.stack([np.where(r <= t, lvl, -1), np.where(r >= t, lvl, -1)]).astype(np.int32)
    return jnp.asarray(a_all, BF16), jnp.asarray(lv)


def _split2(x):
    hi = x.astype(BF16)
    return hi, (x - hi.astype(F32)).astype(BF16)


def _gate_kernel(h_ref, wa1_ref, w2hi_ref, w2lo_ref, ba_ref, g1_ref, g2_ref):
    a_hi, a_lo = _split2(jnp.dot(h_ref[...], wa1_ref[...], preferred_element_type=F32))
    n = g1_ref.shape[1]
    tn = _pick(n, 512)
    for j in range(n // tn):
        cols = slice(j * tn, (j + 1) * tn)
        zz = (jnp.dot(a_hi, w2hi_ref[:, cols], preferred_element_type=F32)
              + jnp.dot(a_lo, w2hi_ref[:, cols], preferred_element_type=F32)
              + jnp.dot(a_hi, w2lo_ref[:, cols], preferred_element_type=F32)) + ba_ref[:, cols]
        g = (jnp.minimum(zz, 0.0) - jnp.log(1.0 + jnp.exp(-jnp.abs(zz)))) * (1.0 / GATE_NORM)
        g1_ref[:, cols], g2_ref[:, cols] = _split2(g)


def _log_gates(hb, wa1, w_layer, wa2, ba):
    m, d = hb.shape
    r, n = wa2.shape
    tm = _pick(m, 256)
    w2hi, w2lo = _split2(wa2)
    full = lambda shape: pl.BlockSpec(shape, lambda i: (0,) * len(shape))
    return pl.pallas_call(
        _gate_kernel,
        grid=(m // tm,),
        in_specs=[pl.BlockSpec((tm, d), lambda i: (i, 0)),
                  pl.BlockSpec((None, d, r), lambda i: (w_layer, 0, 0)),
                  full((r, n)), full((r, n)), full((1, n))],
        out_specs=[pl.BlockSpec((tm, n), lambda i: (i, 0))] * 2,
        out_shape=[jax.ShapeDtypeStruct((m, n), BF16)] * 2,
        compiler_params=_cparams(("parallel",), 48),
        name="log_gates",
    )(hb, wa1, w2hi, w2lo, ba)


def _dot_nt(a, b):
    return lax.dot_general(a, b, (((1,), (1,)), ((), ())), preferred_element_type=F32)


def _gla_chunks(chains, a_all_ref, lv_ref, *, c, hk):
    nlev = c.bit_length() - 1
    n = range(len(chains))
    dirs = [ch[0] for ch in chains]
    edge_row = {0: c - 1, 1: 0}
    ee = [jnp.dot(a_all_ref[dirs[i]], jnp.concatenate(chains[i][4](), axis=0),
                  preferred_element_type=F32) for i in n]
    q = lambda i: chains[i][1]()
    k = lambda i: chains[i][2]()
    scale = hk ** -0.5
    att = [jnp.where(lv_ref[dirs[i]] == 0, _dot_nt(q(i).astype(BF16), k(i).astype(BF16)), 0.0)
           for i in n]
    def level_exponent(i, lev):
        if lev <= GLA_TABLE_LEVELS:
            return ee[i][lev * c:(lev + 1) * c]
        b = ee[i][:c]
        p, ref0 = 1 << lev, (1 << (lev - 1)) - 1 + dirs[i]
        ref = [jnp.broadcast_to(b[p0 + ref0:p0 + ref0 + 1], (p, hk)) for p0 in range(0, c, p)]
        return -jnp.abs(b - (ref[0] if len(ref) == 1 else jnp.concatenate(ref, axis=0)))

    for lev in range(1, nlev + 1):
        for i in n:
            u = jnp.exp(level_exponent(i, lev))
            att[i] = att[i] + jnp.where(
                lv_ref[dirs[i]] == lev,
                _dot_nt((q(i) * u).astype(BF16), (k(i) * u).astype(BF16)), 0.0)
    for i in n:
        b = ee[i][:c]
        vb = chains[i][3]().astype(BF16)
        st = chains[i][5][...]
        o = jnp.dot((q(i) * jnp.exp(b)).astype(BF16), st.astype(BF16), preferred_element_type=F32)
        o = o + jnp.dot(att[i].astype(BF16), vb, preferred_element_type=F32)
        chains[i][6](o * scale)
        er = edge_row[dirs[i]]
        edge = b[er:er + 1]
        kt = jnp.transpose(k(i) * jnp.exp(edge - b)).astype(BF16)
        dcol = jnp.transpose(jnp.broadcast_to(jnp.exp(edge), (128, hk)))[:, :1]
        chains[i][5][...] = dcol * st + jnp.dot(kt, vb, preferred_element_type=F32)


def _gla_kernel(*refs, c, hk, hpb, has_init, emit_final):
    (qf, kf, vf, g1f, g2f, qb, kb, vb, g1b, g2b, aall, lv) = refs[:12]
    pos = 12
    if has_init:
        s0f, s0b = refs[pos:pos + 2]
        pos += 2
    of, ob = refs[pos:pos + 2]
    pos += 2
    if emit_final:
        sff, sfb = refs[pos:pos + 2]
        pos += 2
    sf, sb = refs[pos:pos + 2]
    step = pl.program_id(2)

    @pl.when(step == 0)
    def _():
        if has_init:
            sf[...] = s0f[...]
            sb[...] = s0b[...]
        else:
            sf[...] = jnp.zeros_like(sf)
            sb[...] = jnp.zeros_like(sb)

    hv = of.shape[1] // hpb

    chains = []
    for d, (q, k, v, g1, g2, s, o) in enumerate(((qf, kf, vf, g1f, g2f, sf, of),
                                                 (qb, kb, vb, g1b, g2b, sb, ob))):
        for hh in range(hpb):
            ksl = slice(hh * hk, (hh + 1) * hk)
            vsl = slice(hh * hv, (hh + 1) * hv)

            def store(val, o=o, vsl=vsl):
                o[:, vsl] = val.astype(o.dtype)

            chains.append((d,
                           lambda q=q, ksl=ksl: q[:, ksl],
                           lambda k=k, ksl=ksl: k[:, ksl],
                           lambda v=v, vsl=vsl: v[:, vsl],
                           lambda g1=g1, g2=g2, ksl=ksl: (g1[:, ksl], g2[:, ksl]),
                           s.at[hh], store))
    _gla_chunks(chains, aall, lv, c=c, hk=hk)

    if emit_final:
        @pl.when(step == pl.num_programs(2) - 1)
        def _():
            sff[...] = sf[...]
            sfb[...] = sb[...]


def _gla_scan(z, g1, g2, tables, *, row0, nseq, t, hk, hv,
              layer, n_layers, init=None, final=None, emit_final=False, prev=None):
    m = z.shape[0]
    h = GLA_HEADS
    hpb = GLA_HEADS_PER_STEP
    c = GLA_CHUNK if t % GLA_CHUNK == 0 else t
    nchunk = t // c
    rb0 = row0 // c
    a_all, lv = tables
    fwd = lambda b_, c_: rb0 + b_ * nchunk + c_
    bwd = lambda b_, c_: rb0 + b_ * nchunk + (nchunk - 1 - c_)
    k_col0 = h // hpb
    v_col0 = (2 * h * hk) // (hpb * hv)

    def stream(rowfn, d):
        gate = pl.BlockSpec((c, hpb * hk), lambda b_, h_, c_: (rowfn(b_, c_), d * k_col0 + h_))
        return [pl.BlockSpec((c, hpb * hk), lambda b_, h_, c_: (rowfn(b_, c_), h_)),
                pl.BlockSpec((c, hpb * hk), lambda b_, h_, c_: (rowfn(b_, c_), k_col0 + h_)),
                pl.BlockSpec((c, hpb * hv), lambda b_, h_, c_: (rowfn(b_, c_), v_col0 + h_)),
                gate, gate]

    in_specs = (stream(fwd, 0) + stream(bwd, 1)
                + [pl.BlockSpec(a_all.shape, lambda b_, h_, c_: (0, 0, 0)),
                   pl.BlockSpec(lv.shape, lambda b_, h_, c_: (0, 0, 0))])
    args = [z, z, z, g1, g2, z, z, z, g1, g2, a_all, lv]
    state_spec = pl.BlockSpec((None, None, hpb, hk, hv),
                              lambda b_, h_, c_: (b_, layer, h_, 0, 0))
    if init is not None:
        in_specs += [state_spec, state_spec]
        args += list(init)
    out_specs = [pl.BlockSpec((c, hpb * hv), lambda b_, h_, c_: (fwd(b_, c_), h_)),
                 pl.BlockSpec((c, hpb * hv), lambda b_, h_, c_: (bwd(b_, c_), h_))]
    out_shape = [jax.ShapeDtypeStruct((m, h * hv), BF16)] * 2
    if emit_final:
        out_specs += [state_spec, state_spec]
        out_shape += [jax.ShapeDtypeStruct((nseq, n_layers, h, hk, hv), F32)] * 2
    aliases = {}
    n_blocked = len(args)
    for bufs, out0 in ((prev, 0), (final, 2)):
        if bufs is not None:
            aliases.update({len(args): out0, len(args) + 1: out0 + 1})
            in_specs += [pl.BlockSpec(memory_space=pl.ANY)] * 2
            args += list(bufs)

    def body(*refs):
        refs = refs[:n_blocked] + refs[len(args):]
        _gla_kernel(*refs, c=c, hk=hk, hpb=hpb, has_init=init is not None,
                    emit_final=emit_final)

    return pl.pallas_call(
        body,
        grid=(nseq, h // hpb, nchunk),
        in_specs=in_specs,
        out_specs=out_specs,
        out_shape=out_shape,
        scratch_shapes=[pltpu.VMEM((hpb, hk, hv), F32)] * 2,
        input_output_aliases=aliases,
        compiler_params=_cparams(("parallel", "parallel", "arbitrary"), 56),
        name="gla_scan",
    )(*args)


def _gla_out_kernel(of_ref, ob_ref, r_ref, g_ref, o_ref, *, hv):
    g = g_ref[...]
    for hd in range(GLA_HEADS):
        sl = slice(hd * hv, (hd + 1) * hv)
        o = of_ref[:, sl].astype(F32) + ob_ref[:, sl].astype(F32)
        o = o * lax.rsqrt(jnp.mean(o * o, axis=-1, keepdims=True) + EPS) * g
        r = r_ref[:, sl]
        o_ref[:, sl] = (o * (r * jax.nn.sigmoid(r))).astype(o_ref.dtype)


def _gla_out(o_f, o_b, z, on_g, hv):
    m, d = o_f.shape
    tm = _pick(m, 256)
    r_col = (z.shape[1] - d) // d
    return pl.pallas_call(
        functools.partial(_gla_out_kernel, hv=hv),
        grid=(m // tm,),
        in_specs=[pl.BlockSpec((tm, d), lambda i: (i, 0)),
                  pl.BlockSpec((tm, d), lambda i: (i, 0)),
                  pl.BlockSpec((tm, d), lambda i: (i, r_col)),
                  pl.BlockSpec((1, hv), lambda i: (0, 0))],
        out_specs=pl.BlockSpec((tm, d), lambda i: (i, 0)),
        out_shape=jax.ShapeDtypeStruct((m, d), BF16),
        compiler_params=_cparams(("parallel",), 48),
        name="gla_out",
    )(o_f, o_b, z, on_g.reshape(1, hv))


def _dft_tables(n, sin_sign=1.0):
    idx = jnp.arange(n, dtype=jnp.int32)
    ang = ((idx[:, None] * idx[None, :]) % n).astype(F32) * (2.0 * math.pi / n)
    return jnp.stack([jnp.cos(ang), sin_sign * jnp.sin(ang)]).astype(BF16)


def _chan_dft_kernel(a_ref, w_ref, o_ref):
    o_ref[...] = jnp.dot(a_ref[...], w_ref[...], preferred_element_type=F32).astype(o_ref.dtype)


def _chan_dft(hb, wc):
    m, d = hb.shape
    cg = wc.shape[1]
    tm = _pick(m, 1024)
    return pl.pallas_call(
        _chan_dft_kernel,
        grid=(m // tm, d // cg, 2),
        in_specs=[pl.BlockSpec((tm, cg), lambda i, g, p: (i, g)),
                  pl.BlockSpec((None, cg, cg), lambda i, g, p: (p, 0, 0))],
        out_specs=pl.BlockSpec((None, tm, cg), lambda i, g, p: (p, i, g)),
        out_shape=jax.ShapeDtypeStruct((2, m, d), BF16),
        compiler_params=_cparams(("parallel", "parallel", "arbitrary"), 40),
        name="chan_dft",
    )(hb, wc)


def _seq_dft_kernel(w_ref, p_ref, o_ref, acc_ref, *, nk, scale):
    k = pl.program_id(3)

    def product():
        return jnp.dot(w_ref[...], p_ref[...], preferred_element_type=F32)

    @pl.when(k == 0)
    def _():
        acc_ref[...] = product()

    @pl.when((k > 0) & (k < nk - 1))
    def _():
        acc_ref[...] = acc_ref[...] + product()

    @pl.when(k == nk - 1)
    def _():
        o_ref[...] = ((acc_ref[...] + product()) * scale).astype(o_ref.dtype)


def _seq_dft(planes, wt, y_prev, *, row0, nseq, t, scale):
    _, m, d = planes.shape
    tm, tn, tk = _pick(t, 1024), _pick(d, 1024), _pick(t, 4096)
    kt = t // tk
    nk = 2 * kt
    mt = t // tm
    in_specs = [pl.BlockSpec((None, tm, tk), lambda b, i, j, k: (k // kt, i, k % kt)),
                pl.BlockSpec((None, tk, tn),
                             lambda b, i, j, k: (k // kt, (row0 + b * t) // tk + k % kt, j))]
    args = [wt, planes]
    aliases = {}
    if y_prev is not None:
        in_specs.append(pl.BlockSpec(memory_space=pl.ANY))
        args.append(y_prev)
        aliases = {2: 0}

    def body(w_ref, p_ref, *rest):
        o_ref, acc_ref = rest[-2:]
        _seq_dft_kernel(w_ref, p_ref, o_ref, acc_ref, nk=nk, scale=scale)

    return pl.pallas_call(
        body,
        grid=(nseq, mt, d // tn, nk),
        in_specs=in_specs,
        out_specs=pl.BlockSpec((tm, tn), lambda b, i, j, k: ((row0 + b * t) // tm + i, j)),
        out_shape=jax.ShapeDtypeStruct((m, d), BF16),
        scratch_shapes=[pltpu.VMEM((tm, tn), F32)],
        input_output_aliases=aliases,
        compiler_params=_cparams(("parallel", "parallel", "parallel", "arbitrary"), 56),
        name="seq_dft",
    )(*args)


def _pos_embed_2d(n_tok, d):
    rows = n_tok // GRID_W
    quarter = d // 4
    omega = 1.0 / (10000.0 ** (jnp.arange(quarter, dtype=F32) / quarter))
    er = jnp.arange(rows, dtype=F32)[:, None] * omega
    ec = jnp.arange(GRID_W, dtype=F32)[:, None] * omega
    er = jnp.concatenate([jnp.sin(er), jnp.cos(er)], axis=-1)
    ec = jnp.concatenate([jnp.sin(ec), jnp.cos(ec)], axis=-1)
    half = d // 2
    emb = jnp.concatenate([jnp.broadcast_to(er[:, None, :], (rows, GRID_W, half)),
                           jnp.broadcast_to(ec[None, :, :], (rows, GRID_W, half))], axis=-1)
    return emb.reshape(rows * GRID_W, d)


def kernel(x_prompt, x_sample, state_fwd, state_bwd, c, c_ctx, norm1_g, norm2_g, w_mod, b_mod,
           gla_w_in, gla_wa1_f, gla_wa2_f, gla_ba_f, gla_wa1_b, gla_wa2_b, gla_ba_b,
           gla_onorm_g, gla_w_out, fnet_w, mlp_w1, mlp_w2, final_g):
    batch, seq, d = x_prompt.shape
    dec_batch, dec_seq, _ = x_sample.shape
    depth = w_mod.shape[0]
    n_p, n_l = batch * seq, dec_batch * dec_seq
    h = GLA_HEADS
    hk, hv = gla_wa2_f.shape[2] // h, gla_onorm_g.shape[1]
    rank = gla_wa1_f.shape[2]
    cg = d // F_GROUPS

    n_cond = -(-(1 + dec_batch) // 8) * 8
    cond = jnp.zeros((n_cond, d), F32).at[0].set(c_ctx).at[1:1 + dec_batch].set(c)
    mods = _modulation(cond, w_mod, b_mod)
    mods3 = mods.reshape(depth * n_cond * N_MOD, 1, d)
    rows = _Rows(n_p, dec_seq, n_cond)

    x, hb0 = _embed(x_prompt.reshape(n_p, d), x_sample.reshape(n_l, d), _pos_embed_2d(dec_seq, d),
                    norm1_g[0], mods3, rows)

    tables = _gla_tables(GLA_CHUNK)
    dft_c = _dft_tables(cg)
    dft_tp = _dft_tables(seq, -1.0)
    dft_tl = _dft_tables(dec_seq, -1.0)
    w_in, w_out, w_fnet = (w.astype(BF16) for w in (gla_w_in, gla_w_out, fnet_w))
    w1, w2 = mlp_w1.astype(BF16), mlp_w2.astype(BF16)
    n_gla = gla_w_in.shape[0]
    wa1 = jnp.zeros((n_gla, d, A1_WIDTH), F32)
    wa1 = wa1.at[:, :, :rank].set(gla_wa1_f).at[:, :, rank:2 * rank].set(gla_wa1_b).astype(BF16)
    new_states = None
    for i in range(depth):
        j = i // 2
        hb = hb0 if i == 0 else _norm_mod(x, norm1_g[i], mods3, rows, i, 0)
        resid = lambda which: (x, mods3, rows, i, which)
        if i % 2 == 0:
            z = _matmul(hb, w_in, w_layer=j, out_dtype=F32)
            wa2 = jnp.zeros((A1_WIDTH, 2 * h * hk), F32)
            wa2 = wa2.at[:rank, :h * hk].set(gla_wa2_f[j])
            wa2 = wa2.at[rank:2 * rank, h * hk:].set(gla_wa2_b[j])
            ba = jnp.concatenate([gla_ba_f[j], gla_ba_b[j]]).reshape(1, -1)
            g1, g2 = _log_gates(hb, wa1, j, wa2, ba)
            common = dict(hk=hk, hv=hv, layer=j, n_layers=n_gla)
            o_f, o_b, s_f, s_b = _gla_scan(z, g1, g2, tables, row0=0, nseq=batch, t=seq,
                                           emit_final=True, final=new_states, **common)
            new_states = (s_f, s_b)
            o_f, o_b = _gla_scan(z, g1, g2, tables, row0=n_p, nseq=dec_batch,
                                 t=dec_seq, prev=(o_f, o_b), init=(state_fwd, state_bwd),
                                 **common)
            y = _gla_out(o_f, o_b, z, gla_onorm_g[j], hv)
            x = _matmul(y, w_out, w_layer=j, out_dtype=F32, epilogue="resid", resid=resid(2))
        else:
            planes = _chan_dft(hb, dft_c)
            y = _seq_dft(planes, dft_tp, None, row0=0, nseq=batch, t=seq,
                         scale=1.0 / math.sqrt(seq * cg))
            y = _seq_dft(planes, dft_tl, y, row0=n_p, nseq=dec_batch, t=dec_seq,
                         scale=1.0 / math.sqrt(dec_seq * cg))
            x = _matmul(y, w_fnet, w_layer=j, out_dtype=F32, epilogue="resid", resid=resid(2))
        hb = _norm_mod(x, norm2_g[i], mods3, rows, i, 3)
        a = _matmul(hb, w1, w_layer=i, out_dtype=BF16, epilogue="relu2")
        x = _matmul(a, w2, w_layer=i, out_dtype=F32, epilogue="resid", resid=resid(5))
    y_prompt = _final_norm(x, final_g, 0, n_p).reshape(batch, seq, d)
    y_sample = _final_norm(x, final_g, n_p, n_l).reshape(dec_batch, dec_seq, d)
    return (y_prompt, y_sample) + tuple(new_states)
```

```python
import functools
import math

import numpy as np
import jax
import jax.numpy as jnp
from jax import lax
from jax.experimental import pallas as pl
from jax.experimental.pallas import tpu as pltpu

F32 = jnp.float32
BF16 = jnp.bfloat16

EPS = 1e-6
GLA_HEADS = 8
GATE_NORM = 16.0
LOG2_E = 1.4426950408889634
F_GROUPS = 4
GRID_W = 64
N_MOD = 6
GLA_CHUNK = 128
GLA_HEADS_PER_STEP = 4
GLA_TABLE_LEVELS = 3
A1_WIDTH = 128
NYQ_PAD = 128


def _pick(dim, pref):
    t = min(dim, pref)
    while dim % t:
        t //= 2
    return t


def _cparams(semantics, vmem_mib):
    return pltpu.CompilerParams(dimension_semantics=semantics,
                                vmem_limit_bytes=int(vmem_mib) << 20)


class _Rows:
    def __init__(self, n_p, t_l, n_cond):
        self.n_p, self.t_l, self.n_cond = n_p, t_l, n_cond

    def cond(self, row0):
        return jnp.where(row0 < self.n_p, 0, 1 + (row0 - self.n_p) // self.t_l)

    def mod_row(self, layer, which, row0):
        return (layer * self.n_cond + self.cond(row0)) * N_MOD + which


def _mod_kernel(c_ref, w_ref, b_ref, o_ref):
    c = c_ref[...]
    a = (c * jax.nn.sigmoid(c)).astype(BF16)
    o_ref[...] = jnp.dot(a, w_ref[...].astype(BF16), preferred_element_type=F32) + b_ref[...]


def _modulation(cond, w_mod, b_mod):
    depth, d, n = w_mod.shape
    r = cond.shape[0]
    tn = _pick(n, 512)
    return pl.pallas_call(
        _mod_kernel,
        grid=(depth, n // tn),
        in_specs=[pl.BlockSpec((r, d), lambda l, j: (0, 0)),
                  pl.BlockSpec((None, d, tn), lambda l, j: (l, 0, j)),
                  pl.BlockSpec((None, 1, tn), lambda l, j: (l, 0, j))],
        out_specs=pl.BlockSpec((None, r, tn), lambda l, j: (l, 0, j)),
        out_shape=jax.ShapeDtypeStruct((depth, r, n), F32),
        compiler_params=_cparams(("parallel", "parallel"), 40),
        name="modulation",
    )(cond, w_mod, b_mod.reshape(depth, 1, n))


def _norm_mod_rows(x, g_ref, scale_ref, shift_ref):
    y = x * lax.rsqrt(jnp.mean(x * x, axis=-1, keepdims=True) + EPS)
    return (y * g_ref[...]) * (1.0 + scale_ref[...]) + shift_ref[...]


def _embed_kernel(xp_ref, xl_ref, pos_ref, g_ref, scale_ref, shift_ref, x_ref, h_ref, *, n_pblk):
    i = pl.program_id(0)

    @pl.when(i < n_pblk)
    def _():
        x_ref[...] = xp_ref[...]

    @pl.when(i >= n_pblk)
    def _():
        x_ref[...] = xl_ref[...] + pos_ref[...]

    h_ref[...] = _norm_mod_rows(x_ref[...], g_ref, scale_ref, shift_ref).astype(h_ref.dtype)


def _embed(xp2, xl2, pos, g, mods3, rows):
    n_p, d = xp2.shape
    n_l = xl2.shape[0]
    t_l = pos.shape[0]
    tm = _pick(math.gcd(n_p, t_l), 256)
    n_pblk, n_posblk = n_p // tm, t_l // tm
    mod_spec = lambda which: pl.BlockSpec(
        (None, 1, d), lambda i: (rows.mod_row(0, which, i * tm), 0, 0))
    return pl.pallas_call(
        functools.partial(_embed_kernel, n_pblk=n_pblk),
        grid=((n_p + n_l) // tm,),
        in_specs=[pl.BlockSpec((tm, d), lambda i: (jnp.minimum(i, n_pblk - 1), 0)),
                  pl.BlockSpec((tm, d), lambda i: (jnp.maximum(i - n_pblk, 0), 0)),
                  pl.BlockSpec((tm, d), lambda i: (jnp.maximum(i - n_pblk, 0) % n_posblk, 0)),
                  pl.BlockSpec((1, d), lambda i: (0, 0)),
                  mod_spec(1), mod_spec(0)],
        out_specs=[pl.BlockSpec((tm, d), lambda i: (i, 0))] * 2,
        out_shape=[jax.ShapeDtypeStruct((n_p + n_l, d), F32),
                   jax.ShapeDtypeStruct((n_p + n_l, d), BF16)],
        compiler_params=_cparams(("parallel",), 48),
        name="embed",
    )(xp2, xl2, pos, g.reshape(1, d), mods3, mods3)


def _norm_mod_kernel(x_ref, g_ref, scale_ref, shift_ref, o_ref):
    o_ref[...] = _norm_mod_rows(x_ref[...], g_ref, scale_ref, shift_ref).astype(o_ref.dtype)


def _norm_mod(x, g, mods3, rows, layer, which_shift):
    m, d = x.shape
    tm = _pick(math.gcd(rows.n_p, rows.t_l), 512)
    mod_spec = lambda which: pl.BlockSpec(
        (None, 1, d), lambda i: (rows.mod_row(layer, which, i * tm), 0, 0))
    return pl.pallas_call(
        _norm_mod_kernel,
        grid=(m // tm,),
        in_specs=[pl.BlockSpec((tm, d), lambda i: (i, 0)),
                  pl.BlockSpec((1, d), lambda i: (0, 0)),
                  mod_spec(which_shift + 1), mod_spec(which_shift)],
        out_specs=pl.BlockSpec((tm, d), lambda i: (i, 0)),
        out_shape=jax.ShapeDtypeStruct((m, d), BF16),
        compiler_params=_cparams(("parallel",), 48),
        name="norm_mod",
    )(x, g.reshape(1, d), mods3, mods3)


def _final_norm_kernel(x_ref, g_ref, o_ref):
    x = x_ref[...]
    y = x * lax.rsqrt(jnp.mean(x * x, axis=-1, keepdims=True) + EPS)
    o_ref[...] = y * g_ref[...]


def _final_norm(x, g, row0, nrows):
    d = x.shape[1]
    tm = _pick(math.gcd(row0, nrows) if row0 else nrows, 512)
    blk0 = row0 // tm
    return pl.pallas_call(
        _final_norm_kernel,
        grid=(nrows // tm,),
        in_specs=[pl.BlockSpec((tm, d), lambda i: (blk0 + i, 0)),
                  pl.BlockSpec((1, d), lambda i: (0, 0))],
        out_specs=pl.BlockSpec((tm, d), lambda i: (i, 0)),
        out_shape=jax.ShapeDtypeStruct((nrows, d), F32),
        compiler_params=_cparams(("parallel",), 48),
        name="final_norm",
    )(x, g.reshape(1, d))


def _mm_kernel(*refs, nk, epilogue):
    if epilogue == "resid":
        a_ref, b_ref, x_ref, gate_ref, o_ref = refs[:5]
        scratch = refs[5:]
    else:
        a_ref, b_ref, o_ref = refs[:3]
        scratch = refs[3:]

    def finish(acc):
        if epilogue == "relu2":
            r = jnp.maximum(acc, 0.0)
            o_ref[...] = (r * r).astype(o_ref.dtype)
        elif epilogue == "resid":
            o_ref[...] = x_ref[...] + gate_ref[...] * acc
        else:
            o_ref[...] = acc.astype(o_ref.dtype)

    def product():
        return jnp.dot(a_ref[...], b_ref[...], preferred_element_type=F32)

    if nk == 1:
        finish(product())
        return
    acc_ref = o_ref if epilogue == "resid" else scratch[0]
    k = pl.program_id(2)

    @pl.when(k == 0)
    def _():
        acc_ref[...] = product()

    @pl.when((k > 0) & (k < nk - 1))
    def _():
        acc_ref[...] = acc_ref[...] + product()

    @pl.when(k == nk - 1)
    def _():
        finish(acc_ref[...] + product())


def _matmul(a, b, *, out_dtype, epilogue="plain", resid=None, w_layer=0,
            tm=1024, tn=1024, tk=4096):
    planes = a.shape[0] if a.ndim == 3 else 1
    m, kdim = a.shape[-2:]
    n = b.shape[2]
    tm, tn, tk = _pick(m, tm), _pick(n, tn), _pick(kdim, tk)
    if resid is not None:
        tm = _pick(math.gcd(resid[2].n_p, resid[2].t_l), tm)
    kp = kdim // tk
    nk = planes * kp
    if a.ndim == 3:
        in_specs = [pl.BlockSpec((None, tm, tk), lambda i, j, k: (k // kp, i, k % kp)),
                    pl.BlockSpec((None, tk, tn),
                                 lambda i, j, k: (w_layer * planes + k // kp, k % kp, j))]
    else:
        in_specs = [pl.BlockSpec((tm, tk), lambda i, j, k: (i, k)),
                    pl.BlockSpec((None, tk, tn), lambda i, j, k: (w_layer, k, j))]
    args = [a, b]
    if epilogue == "resid":
        x, mods3, rows, layer, which = resid
        in_specs += [pl.BlockSpec((tm, tn), lambda i, j, k: (i, j)),
                     pl.BlockSpec((None, 1, tn),
                                  lambda i, j, k: (rows.mod_row(layer, which, i * tm), 0, j))]
        args += [x, mods3]
    out_bytes = jnp.dtype(out_dtype).itemsize
    b_bytes = jnp.dtype(b.dtype).itemsize
    vmem = 2 * (tm * tk * 2 + tk * tn * b_bytes) + 2 * tm * tn * out_bytes
    if b.dtype != BF16:
        vmem += tk * tn * 2
    if epilogue == "resid":
        vmem += 2 * tm * tn * 4
    use_acc = nk > 1 and epilogue != "resid"
    if use_acc:
        vmem += tm * tn * 4
    vmem += 2 * tm * tn * 4
    return pl.pallas_call(
        functools.partial(_mm_kernel, nk=nk, epilogue=epilogue),
        grid=(m // tm, n // tn, nk),
        in_specs=in_specs,
        out_specs=pl.BlockSpec((tm, tn), lambda i, j, k: (i, j)),
        out_shape=jax.ShapeDtypeStruct((m, n), out_dtype),
        scratch_shapes=[pltpu.VMEM((tm, tn), F32)] if use_acc else [],
        compiler_params=_cparams(("parallel", "parallel", "arbitrary"),
                                 min(60, vmem / 2 ** 20 + 6)),
        name="matmul_" + epilogue,
    )(*args)


def _gla_tables(c):
    t = np.arange(c)[:, None]
    r = np.arange(c)[None, :]
    a_f = [(r <= t)]
    a_b = [(r >= t)]
    for lev in range(1, GLA_TABLE_LEVELS + 1):
        p, hh = 1 << lev, 1 << (lev - 1)
        mid = (t // p) * p + hh
        left = (t % p) < hh
        a_f.append(np.where(left, (r > t) & (r <= mid - 1), (r >= mid) & (r <= t)))
        a_b.append(np.where(left, (r >= t) & (r < mid), (r >= mid) & (r < t)))
    a_all = np.stack([np.concatenate(a_f, 0), np.concatenate(a_b, 0)]).astype(np.float32)
    a_all = np.tile(a_all, (1, 1, 2))
    x = np.bitwise_xor(t, r)
    lvl = np.where(x == 0, 0, np.floor(np.log2(np.maximum(x, 1))).astype(np.int64) + 1)
    lv = np.stack([np.where(r <= t, lvl, -1), np.where(r >= t, lvl, -1)]).astype(np.int32)
    return jnp.asarray(a_all, BF16), jnp.asarray(lv)


def _split2(x):
    hi = x.astype(BF16)
    return hi, (x - hi.astype(F32)).astype(BF16)


def _gate_kernel(h_ref, wa1_ref, w2hi_ref, w2lo_ref, ba_ref, g1_ref, g2_ref):
    a_hi, a_lo = _split2(jnp.dot(h_ref[...], wa1_ref[...], preferred_element_type=F32))
    n = g1_ref.shape[1]
    tn = _pick(n, 512)
    for j in range(n // tn):
        cols = slice(j * tn, (j + 1) * tn)
        zz = (jnp.dot(a_hi, w2hi_ref[:, cols], preferred_element_type=F32)
              + jnp.dot(a_lo, w2hi_ref[:, cols], preferred_element_type=F32)
              + jnp.dot(a_hi, w2lo_ref[:, cols], preferred_element_type=F32)) + ba_ref[:, cols]
        g = ((jnp.minimum(zz, 0.0) - jnp.log(1.0 + jnp.exp(-jnp.abs(zz))))
             * (LOG2_E / GATE_NORM))
        g1_ref[:, cols], g2_ref[:, cols] = _split2(g)


def _log_gates(hb, wa1, w_layer, wa2, ba):
    m, d = hb.shape
    r, n = wa2.shape
    tm = _pick(m, 256)
    w2hi, w2lo = _split2(wa2)
    full = lambda shape: pl.BlockSpec(shape, lambda i: (0,) * len(shape))
    return pl.pallas_call(
        _gate_kernel,
        grid=(m // tm,),
        in_specs=[pl.BlockSpec((tm, d), lambda i: (i, 0)),
                  pl.BlockSpec((None, d, r), lambda i: (w_layer, 0, 0)),
                  full((r, n)), full((r, n)), full((1, n))],
        out_specs=[pl.BlockSpec((tm, n), lambda i: (i, 0))] * 2,
        out_shape=[jax.ShapeDtypeStruct((m, n), BF16)] * 2,
        compiler_params=_cparams(("parallel",), 48),
        name="log_gates",
    )(hb, wa1, w2hi, w2lo, ba)


def _dot_nt(a, b):
    return lax.dot_general(a, b, (((1,), (1,)), ((), ())), preferred_element_type=F32)


def _gla_chunks(chains, a_all_ref, lv_ref, *, c, hk):
    nlev = c.bit_length() - 1
    n = range(len(chains))
    dirs = [ch[0] for ch in chains]
    edge_row = {0: c - 1, 1: 0}
    ee = [jnp.dot(a_all_ref[dirs[i]], jnp.concatenate(chains[i][4](), axis=0),
                  preferred_element_type=F32) for i in n]
    q = lambda i: chains[i][1]()
    k = lambda i: chains[i][2]()
    scale = hk ** -0.5
    qb = [q(i).astype(BF16) for i in n]
    kb = [k(i).astype(BF16) for i in n]
    att = [jnp.where(lv_ref[dirs[i]] == 0, _dot_nt(qb[i], kb[i]), 0.0) for i in n]
    def level_exponent(i, lev):
        if lev <= GLA_TABLE_LEVELS:
            return ee[i][lev * c:(lev + 1) * c]
        b = ee[i][:c]
        p, ref0 = 1 << lev, (1 << (lev - 1)) - 1 + dirs[i]
        ref = [jnp.broadcast_to(b[p0 + ref0:p0 + ref0 + 1], (p, hk)) for p0 in range(0, c, p)]
        return -jnp.abs(b - (ref[0] if len(ref) == 1 else jnp.concatenate(ref, axis=0)))

    for lev in range(1, nlev + 1):
        for i in n:
            u = jnp.exp2(level_exponent(i, lev)).astype(BF16)
            att[i] = att[i] + jnp.where(lv_ref[dirs[i]] == lev,
                                        _dot_nt(qb[i] * u, kb[i] * u), 0.0)
    for i in n:
        b = ee[i][:c]
        vb = chains[i][3]().astype(BF16)
        st = chains[i][5][...]
        o = jnp.dot((q(i) * jnp.exp2(b)).astype(BF16), st.astype(BF16), preferred_element_type=F32)
        o = o + jnp.dot(att[i].astype(BF16), vb, preferred_element_type=F32)
        chains[i][6](o * scale)
        er = edge_row[dirs[i]]
        edge = b[er:er + 1]
        kt = jnp.transpose(k(i) * jnp.exp2(edge - b)).astype(BF16)
        dcol = jnp.transpose(jnp.broadcast_to(jnp.exp2(edge), (128, hk)))[:, :1]
        chains[i][5][...] = dcol * st + jnp.dot(kt, vb, preferred_element_type=F32)


def _gla_kernel(*refs, c, hk, hpb, has_init, emit_final):
    (qf, kf, vf, g1f, g2f, qb, kb, vb, g1b, g2b, aall, lv) = refs[:12]
    pos = 12
    if has_init:
        s0f, s0b = refs[pos:pos + 2]
        pos += 2
    of, ob = refs[pos:pos + 2]
    pos += 2
    if emit_final:
        sff, sfb = refs[pos:pos + 2]
        pos += 2
    sf, sb = refs[pos:pos + 2]
    step = pl.program_id(2)

    @pl.when(step == 0)
    def _():
        if has_init:
            sf[...] = s0f[...]
            sb[...] = s0b[...]
        else:
            sf[...] = jnp.zeros_like(sf)
            sb[...] = jnp.zeros_like(sb)

    hv = of.shape[1] // hpb

    chains = []
    for d, (q, k, v, g1, g2, s, o) in enumerate(((qf, kf, vf, g1f, g2f, sf, of),
                                                 (qb, kb, vb, g1b, g2b, sb, ob))):
        for hh in range(hpb):
            ksl = slice(hh * hk, (hh + 1) * hk)
            vsl = slice(hh * hv, (hh + 1) * hv)

            def store(val, o=o, vsl=vsl):
                o[:, vsl] = val.astype(o.dtype)

            chains.append((d,
                           lambda q=q, ksl=ksl: q[:, ksl],
                           lambda k=k, ksl=ksl: k[:, ksl],
                           lambda v=v, vsl=vsl: v[:, vsl],
                           lambda g1=g1, g2=g2, ksl=ksl: (g1[:, ksl], g2[:, ksl]),
                           s.at[hh], store))
    _gla_chunks(chains, aall, lv, c=c, hk=hk)

    if emit_final:
        @pl.when(step == pl.num_programs(2) - 1)
        def _():
            sff[...] = sf[...]
            sfb[...] = sb[...]


def _gla_scan(z, g1, g2, tables, *, row0, nseq, t, hk, hv,
              layer, n_layers, init=None, final=None, emit_final=False, prev=None):
    m = z.shape[0]
    h = GLA_HEADS
    hpb = GLA_HEADS_PER_STEP
    c = GLA_CHUNK if t % GLA_CHUNK == 0 else t
    nchunk = t // c
    rb0 = row0 // c
    a_all, lv = tables
    fwd = lambda b_, c_: rb0 + b_ * nchunk + c_
    bwd = lambda b_, c_: rb0 + b_ * nchunk + (nchunk - 1 - c_)
    k_col0 = h // hpb
    v_col0 = (2 * h * hk) // (hpb * hv)

    def stream(rowfn, d):
        gate = pl.BlockSpec((c, hpb * hk), lambda b_, h_, c_: (rowfn(b_, c_), d * k_col0 + h_))
        return [pl.BlockSpec((c, hpb * hk), lambda b_, h_, c_: (rowfn(b_, c_), h_)),
                pl.BlockSpec((c, hpb * hk), lambda b_, h_, c_: (rowfn(b_, c_), k_col0 + h_)),
                pl.BlockSpec((c, hpb * hv), lambda b_, h_, c_: (rowfn(b_, c_), v_col0 + h_)),
                gate, gate]

    in_specs = (stream(fwd, 0) + stream(bwd, 1)
                + [pl.BlockSpec(a_all.shape, lambda b_, h_, c_: (0, 0, 0)),
                   pl.BlockSpec(lv.shape, lambda b_, h_, c_: (0, 0, 0))])
    args = [z, z, z, g1, g2, z, z, z, g1, g2, a_all, lv]
    state_spec = pl.BlockSpec((None, None, hpb, hk, hv),
                              lambda b_, h_, c_: (b_, layer, h_, 0, 0))
    if init is not None:
        in_specs += [state_spec, state_spec]
        args += list(init)
    out_specs = [pl.BlockSpec((c, hpb * hv), lambda b_, h_, c_: (fwd(b_, c_), h_)),
                 pl.BlockSpec((c, hpb * hv), lambda b_, h_, c_: (bwd(b_, c_), h_))]
    out_shape = [jax.ShapeDtypeStruct((m, h * hv), BF16)] * 2
    if emit_final:
        out_specs += [state_spec, state_spec]
        out_shape += [jax.ShapeDtypeStruct((nseq, n_layers, h, hk, hv), F32)] * 2
    aliases = {}
    n_blocked = len(args)
    for bufs, out0 in ((prev, 0), (final, 2)):
        if bufs is not None:
            aliases.update({len(args): out0, len(args) + 1: out0 + 1})
            in_specs += [pl.BlockSpec(memory_space=pl.ANY)] * 2
            args += list(bufs)

    def body(*refs):
        refs = refs[:n_blocked] + refs[len(args):]
        _gla_kernel(*refs, c=c, hk=hk, hpb=hpb, has_init=init is not None,
                    emit_final=emit_final)

    return pl.pallas_call(
        body,
        grid=(nseq, h // hpb, nchunk),
        in_specs=in_specs,
        out_specs=out_specs,
        out_shape=out_shape,
        scratch_shapes=[pltpu.VMEM((hpb, hk, hv), F32)] * 2,
        input_output_aliases=aliases,
        compiler_params=_cparams(("parallel", "parallel", "arbitrary"), 56),
        name="gla_scan",
    )(*args)


def _gla_out_kernel(of_ref, ob_ref, r_ref, g_ref, o_ref, *, hv):
    g = g_ref[...]
    for hd in range(GLA_HEADS):
        sl = slice(hd * hv, (hd + 1) * hv)
        o = of_ref[:, sl].astype(F32) + ob_ref[:, sl].astype(F32)
        o = o * lax.rsqrt(jnp.mean(o * o, axis=-1, keepdims=True) + EPS) * g
        r = r_ref[:, sl]
        o_ref[:, sl] = (o * (r * jax.nn.sigmoid(r))).astype(o_ref.dtype)


def _gla_out(o_f, o_b, z, on_g, hv):
    m, d = o_f.shape
    tm = _pick(m, 256)
    r_col = (z.shape[1] - d) // d
    return pl.pallas_call(
        functools.partial(_gla_out_kernel, hv=hv),
        grid=(m // tm,),
        in_specs=[pl.BlockSpec((tm, d), lambda i: (i, 0)),
                  pl.BlockSpec((tm, d), lambda i: (i, 0)),
                  pl.BlockSpec((tm, d), lambda i: (i, r_col)),
                  pl.BlockSpec((1, hv), lambda i: (0, 0))],
        out_specs=pl.BlockSpec((tm, d), lambda i: (i, 0)),
        out_shape=jax.ShapeDtypeStruct((m, d), BF16),
        compiler_params=_cparams(("parallel",), 48),
        name="gla_out",
    )(o_f, o_b, z, on_g.reshape(1, hv))


def _dft_tables(n):
    idx = jnp.arange(n, dtype=jnp.int32)
    ang = ((idx[:, None] * idx[None, :]) % n).astype(F32) * (2.0 * math.pi / n)
    return jnp.stack([jnp.cos(ang), jnp.sin(ang)]).astype(BF16)


def _chan_table(cg):
    half = cg // 2
    cs = _dft_tables(cg)[:, :, :half]
    alt = jnp.where(jnp.arange(cg) % 2 == 0, 1.0, -1.0).astype(BF16)
    nyq = jnp.zeros((cg, NYQ_PAD), BF16).at[:, 0].set(alt)
    return jnp.concatenate([cs[0], cs[1], nyq], axis=1)


def _mirror_rows(d, cg):
    half = cg // 2
    base = (np.arange(d // cg) * cg)[:, None]
    f = np.arange(half)[None, :]
    mirror = np.where(f == 0, half, (cg - f) % cg)
    return np.concatenate([(base + f).reshape(-1), (base + mirror).reshape(-1)])


def _chan_dft_kernel(a_ref, w_ref, o_ref, n_ref, *, half):
    r = jnp.dot(a_ref[...], w_ref[...], preferred_element_type=F32)
    o_ref[0] = r[:, :half].astype(o_ref.dtype)
    o_ref[1] = r[:, half:2 * half].astype(o_ref.dtype)
    n_ref[...] = r[:, 2 * half:].astype(n_ref.dtype)


def _chan_dft(hb, table):
    m, d = hb.shape
    cg = table.shape[0]
    half, groups = cg // 2, d // cg
    tm = _pick(m, 1024)
    return pl.pallas_call(
        functools.partial(_chan_dft_kernel, half=half),
        grid=(m // tm, groups),
        in_specs=[pl.BlockSpec((tm, cg), lambda i, g: (i, g)),
                  pl.BlockSpec(table.shape, lambda i, g: (0, 0))],
        out_specs=[pl.BlockSpec((2, tm, half), lambda i, g: (0, i, g)),
                   pl.BlockSpec((tm, NYQ_PAD), lambda i, g: (i, g))],
        out_shape=[jax.ShapeDtypeStruct((2, m, groups * half), BF16),
                   jax.ShapeDtypeStruct((m, groups * NYQ_PAD), BF16)],
        compiler_params=_cparams(("parallel", "arbitrary"), 40),
        name="chan_dft",
    )(hb, table)


def _seq_dft_whole_kernel(w_ref, p0_ref, p1_ref, xn_ref, o_ref, *, half, scale):
    cp = jnp.dot(w_ref[0], p0_ref[...], preferred_element_type=F32)
    sp = jnp.dot(w_ref[1], p1_ref[...], preferred_element_type=F32)
    cn = jnp.dot(w_ref[0], xn_ref[...], preferred_element_type=F32)
    o_ref[0] = ((cp - sp) * scale).astype(o_ref.dtype)
    o_ref[1] = ((cp + sp) * scale).astype(o_ref.dtype)
    for g in range(o_ref.shape[2] // half):
        o_ref[1, :, g * half:g * half + 1] = (
            cn[:, g * NYQ_PAD:g * NYQ_PAD + 1] * scale).astype(o_ref.dtype)


def _seq_dft_whole(planes, nyq, wt, *, row0, nseq, t, scale, half):
    _, m, n = planes.shape
    assert row0 % t == 0, "sequences must start on a multiple of their length"
    blk0 = row0 // t
    return pl.pallas_call(
        functools.partial(_seq_dft_whole_kernel, half=half, scale=scale),
        grid=(nseq,),
        in_specs=[pl.BlockSpec(wt.shape, lambda b: (0, 0, 0)),
                  pl.BlockSpec((None, t, n), lambda b: (0, blk0 + b, 0)),
                  pl.BlockSpec((None, t, n), lambda b: (1, blk0 + b, 0)),
                  pl.BlockSpec((t, nyq.shape[1]), lambda b: (blk0 + b, 0))],
        out_specs=pl.BlockSpec((2, t, n), lambda b: (0, blk0 + b, 0)),
        out_shape=jax.ShapeDtypeStruct((2, m, n), BF16),
        compiler_params=_cparams(("parallel",), 40),
        name="seq_dft_whole",
    )(wt, planes, planes, nyq)


def _seq_dft_tiled_kernel(w_ref, p_ref, xn_ref, o_ref, cp_ref, cn_ref, *, scale):
    k, g = pl.program_id(2), pl.program_id(3)

    @pl.when(k == 0)
    def _():
        cp_ref[g] = jnp.dot(w_ref[...], p_ref[...], preferred_element_type=F32)
        cn_ref[g] = jnp.dot(w_ref[...], xn_ref[...], preferred_element_type=F32)

    @pl.when(k == 1)
    def _():
        sp = jnp.dot(w_ref[...], p_ref[...], preferred_element_type=F32)
        cp = cp_ref[g]
        o_ref[0] = ((cp - sp) * scale).astype(o_ref.dtype)
        o_ref[1] = ((cp + sp) * scale).astype(o_ref.dtype)
        o_ref[1, :, 0:1] = (cn_ref[g][:, 0:1] * scale).astype(o_ref.dtype)


def _seq_dft_tiled(planes, nyq, wt, y_prev, *, row0, nseq, t, scale, half):
    _, m, n = planes.shape
    groups = n // half
    tm = _pick(t, 1024)
    assert row0 % t == 0, "sequences must start on a multiple of their length"
    blk_t = lambda b: (row0 + b * t) // t
    blk_m = lambda b, i: (row0 + b * t) // tm + i
    return pl.pallas_call(
        lambda w, p, xn, y_any, o, cp, cn: _seq_dft_tiled_kernel(w, p, xn, o, cp, cn, scale=scale),
        grid=(nseq, t // tm, 2, groups),
        in_specs=[pl.BlockSpec((None, tm, t), lambda b, i, k, g: (k, i, 0)),
                  pl.BlockSpec((None, t, half), lambda b, i, k, g: (k, blk_t(b), g)),
                  pl.BlockSpec((t, NYQ_PAD), lambda b, i, k, g: (blk_t(b), g)),
                  pl.BlockSpec(memory_space=pl.ANY)],
        out_specs=pl.BlockSpec((2, tm, half), lambda b, i, k, g: (0, blk_m(b, i), g * k)),
        out_shape=jax.ShapeDtypeStruct((2, m, n), BF16),
        scratch_shapes=[pltpu.VMEM((groups, tm, half), F32),
                        pltpu.VMEM((groups, tm, NYQ_PAD), F32)],
        input_output_aliases={3: 0},
        compiler_params=_cparams(("parallel", "parallel", "arbitrary", "arbitrary"), 56),
        name="seq_dft_tiled",
    )(wt, planes, nyq, y_prev)


def _pos_embed_2d(n_tok, d):
    rows = n_tok // GRID_W
    quarter = d // 4
    omega = 1.0 / (10000.0 ** (jnp.arange(quarter, dtype=F32) / quarter))
    er = jnp.arange(rows, dtype=F32)[:, None] * omega
    ec = jnp.arange(GRID_W, dtype=F32)[:, None] * omega
    er = jnp.concatenate([jnp.sin(er), jnp.cos(er)], axis=-1)
    ec = jnp.concatenate([jnp.sin(ec), jnp.cos(ec)], axis=-1)
    half = d // 2
    emb = jnp.concatenate([jnp.broadcast_to(er[:, None, :], (rows, GRID_W, half)),
                           jnp.broadcast_to(ec[None, :, :], (rows, GRID_W, half))], axis=-1)
    return emb.reshape(rows * GRID_W, d)


def kernel(x_prompt, x_sample, state_fwd, state_bwd, c, c_ctx, norm1_g, norm2_g, w_mod, b_mod,
           gla_w_in, gla_wa1_f, gla_wa2_f, gla_ba_f, gla_wa1_b, gla_wa2_b, gla_ba_b,
           gla_onorm_g, gla_w_out, fnet_w, mlp_w1, mlp_w2, final_g):
    batch, seq, d = x_prompt.shape
    dec_batch, dec_seq, _ = x_sample.shape
    depth = w_mod.shape[0]
    n_p, n_l = batch * seq, dec_batch * dec_seq
    h = GLA_HEADS
    hk, hv = gla_wa2_f.shape[2] // h, gla_onorm_g.shape[1]
    rank = gla_wa1_f.shape[2]
    cg = d // F_GROUPS

    n_cond = -(-(1 + dec_batch) // 8) * 8
    cond = jnp.zeros((n_cond, d), F32).at[0].set(c_ctx).at[1:1 + dec_batch].set(c)
    mods = _modulation(cond, w_mod, b_mod)
    mods3 = mods.reshape(depth * n_cond * N_MOD, 1, d)
    rows = _Rows(n_p, dec_seq, n_cond)

    x, hb0 = _embed(x_prompt.reshape(n_p, d), x_sample.reshape(n_l, d), _pos_embed_2d(dec_seq, d),
                    norm1_g[0], mods3, rows)

    tables = _gla_tables(GLA_CHUNK)
    dft_c = _chan_table(cg)
    dft_tp = _dft_tables(seq)
    dft_tl = _dft_tables(dec_seq)
    w_in, w_out = gla_w_in.astype(BF16), gla_w_out.astype(BF16)
    w_fnet = jnp.take(fnet_w, _mirror_rows(d, cg), axis=1).astype(BF16)
    w_fnet = w_fnet.reshape(2 * fnet_w.shape[0], d // 2, d)
    w1, w2 = mlp_w1.astype(BF16), mlp_w2.astype(BF16)
    n_gla = gla_w_in.shape[0]
    wa1 = jnp.zeros((n_gla, d, A1_WIDTH), F32)
    wa1 = wa1.at[:, :, :rank].set(gla_wa1_f).at[:, :, rank:2 * rank].set(gla_wa1_b).astype(BF16)
    new_states = None
    for i in range(depth):
        j = i // 2
        hb = hb0 if i == 0 else _norm_mod(x, norm1_g[i], mods3, rows, i, 0)
        resid = lambda which: (x, mods3, rows, i, which)
        if i % 2 == 0:
            z = _matmul(hb, w_in, w_layer=j, out_dtype=F32)
            wa2 = jnp.zeros((A1_WIDTH, 2 * h * hk), F32)
            wa2 = wa2.at[:rank, :h * hk].set(gla_wa2_f[j])
            wa2 = wa2.at[rank:2 * rank, h * hk:].set(gla_wa2_b[j])
            ba = jnp.concatenate([gla_ba_f[j], gla_ba_b[j]]).reshape(1, -1)
            g1, g2 = _log_gates(hb, wa1, j, wa2, ba)
            common = dict(hk=hk, hv=hv, layer=j, n_layers=n_gla)
            o_f, o_b, s_f, s_b = _gla_scan(z, g1, g2, tables, row0=0, nseq=batch, t=seq,
                                           emit_final=True, final=new_states, **common)
            new_states = (s_f, s_b)
            o_f, o_b = _gla_scan(z, g1, g2, tables, row0=n_p, nseq=dec_batch,
                                 t=dec_seq, prev=(o_f, o_b), init=(state_fwd, state_bwd),
                                 **common)
            y = _gla_out(o_f, o_b, z, gla_onorm_g[j], hv)
            x = _matmul(y, w_out, w_layer=j, out_dtype=F32, epilogue="resid", resid=resid(2))
        else:
            planes, nyq = _chan_dft(hb, dft_c)
            y = _seq_dft_whole(planes, nyq, dft_tp, row0=0, nseq=batch, t=seq, half=cg // 2,
                               scale=1.0 / math.sqrt(seq * cg))
            y = _seq_dft_tiled(planes, nyq, dft_tl, y, row0=n_p, nseq=dec_batch, t=dec_seq,
                               half=cg // 2, scale=1.0 / math.sqrt(dec_seq * cg))
            x = _matmul(y, w_fnet, w_layer=j, out_dtype=F32, epilogue="resid", resid=resid(2))
        hb = _norm_mod(x, norm2_g[i], mods3, rows, i, 3)
        a = _matmul(hb, w1, w_layer=i, out_dtype=BF16, epilogue="relu2")
        x = _matmul(a, w2, w_layer=i, out_dtype=F32, epilogue="resid", resid=resid(5))
    y_prompt = _final_norm(x, final_g, 0, n_p).reshape(batch, seq, d)
    y_sample = _final_norm(x, final_g, n_p, n_l).reshape(dec_batch, dec_seq, d)
    return (y_prompt, y_sample) + tuple(new_states)
```

```python
import functools
import math

import numpy as np
import jax
import jax.numpy as jnp
from jax import lax
from jax.experimental import pallas as pl
from jax.experimental.pallas import tpu as pltpu

F32 = jnp.float32
BF16 = jnp.bfloat16

EPS = 1e-6
GLA_HEADS = 8
GATE_NORM = 16.0
LOG2_E = 1.4426950408889634
F_GROUPS = 4
GRID_W = 64
N_MOD = 6
GLA_CHUNK = 128
GLA_HEADS_PER_STEP = 4
GLA_TABLE_LEVELS = 3
A1_WIDTH = 128
NYQ_PAD = 128


def _pick(dim, pref):
    t = min(dim, pref)
    while dim % t:
        t //= 2
    return t


def _cparams(semantics, vmem_mib):
    return pltpu.CompilerParams(dimension_semantics=semantics,
                                vmem_limit_bytes=int(vmem_mib) << 20)


class _Rows:
    def __init__(self, n_p, t_l, n_cond):
        self.n_p, self.t_l, self.n_cond = n_p, t_l, n_cond

    def cond(self, row0):
        return jnp.where(row0 < self.n_p, 0, 1 + (row0 - self.n_p) // self.t_l)

    def mod_row(self, layer, which, row0):
        return (layer * self.n_cond + self.cond(row0)) * N_MOD + which


def _mod_kernel(c_ref, w_ref, b_ref, o_ref):
    c = c_ref[...]
    a = (c * jax.nn.sigmoid(c)).astype(BF16)
    o_ref[...] = jnp.dot(a, w_ref[...].astype(BF16), preferred_element_type=F32) + b_ref[...]


def _modulation(cond, w_mod, b_mod):
    depth, d, n = w_mod.shape
    r = cond.shape[0]
    tn = _pick(n, 512)
    return pl.pallas_call(
        _mod_kernel,
        grid=(depth, n // tn),
        in_specs=[pl.BlockSpec((r, d), lambda l, j: (0, 0)),
                  pl.BlockSpec((None, d, tn), lambda l, j: (l, 0, j)),
                  pl.BlockSpec((None, 1, tn), lambda l, j: (l, 0, j))],
        out_specs=pl.BlockSpec((None, r, tn), lambda l, j: (l, 0, j)),
        out_shape=jax.ShapeDtypeStruct((depth, r, n), F32),
        compiler_params=_cparams(("parallel", "parallel"), 40),
        name="modulation",
    )(cond, w_mod, b_mod.reshape(depth, 1, n))


def _norm_mod_rows(x, g_ref, scale_ref, shift_ref):
    y = x * lax.rsqrt(jnp.mean(x * x, axis=-1, keepdims=True) + EPS)
    return (y * g_ref[...]) * (1.0 + scale_ref[...]) + shift_ref[...]


def _embed_kernel(xp_ref, xl_ref, pos_ref, g_ref, scale_ref, shift_ref, x_ref, h_ref, *, n_pblk):
    i = pl.program_id(0)

    @pl.when(i < n_pblk)
    def _():
        x_ref[...] = xp_ref[...]

    @pl.when(i >= n_pblk)
    def _():
        x_ref[...] = xl_ref[...] + pos_ref[...]

    h_ref[...] = _norm_mod_rows(x_ref[...], g_ref, scale_ref, shift_ref).astype(h_ref.dtype)


def _embed(xp2, xl2, pos, g, mods3, rows):
    n_p, d = xp2.shape
    n_l = xl2.shape[0]
    t_l = pos.shape[0]
    tm = _pick(math.gcd(n_p, t_l), 256)
    n_pblk, n_posblk = n_p // tm, t_l // tm
    mod_spec = lambda which: pl.BlockSpec(
        (None, 1, d), lambda i: (rows.mod_row(0, which, i * tm), 0, 0))
    return pl.pallas_call(
        functools.partial(_embed_kernel, n_pblk=n_pblk),
        grid=((n_p + n_l) // tm,),
        in_specs=[pl.BlockSpec((tm, d), lambda i: (jnp.minimum(i, n_pblk - 1), 0)),
                  pl.BlockSpec((tm, d), lambda i: (jnp.maximum(i - n_pblk, 0), 0)),
                  pl.BlockSpec((tm, d), lambda i: (jnp.maximum(i - n_pblk, 0) % n_posblk, 0)),
                  pl.BlockSpec((1, d), lambda i: (0, 0)),
                  mod_spec(1), mod_spec(0)],
        out_specs=[pl.BlockSpec((tm, d), lambda i: (i, 0))] * 2,
        out_shape=[jax.ShapeDtypeStruct((n_p + n_l, d), F32),
                   jax.ShapeDtypeStruct((n_p + n_l, d), BF16)],
        compiler_params=_cparams(("parallel",), 48),
        name="embed",
    )(xp2, xl2, pos, g.reshape(1, d), mods3, mods3)


def _norm_mod_kernel(x_ref, g_ref, scale_ref, shift_ref, o_ref):
    o_ref[...] = _norm_mod_rows(x_ref[...], g_ref, scale_ref, shift_ref).astype(o_ref.dtype)


def _norm_mod(x, g, mods3, rows, layer, which_shift):
    m, d = x.shape
    tm = _pick(math.gcd(rows.n_p, rows.t_l), 512)
    mod_spec = lambda which: pl.BlockSpec(
        (None, 1, d), lambda i: (rows.mod_row(layer, which, i * tm), 0, 0))
    return pl.pallas_call(
        _norm_mod_kernel,
        grid=(m // tm,),
        in_specs=[pl.BlockSpec((tm, d), lambda i: (i, 0)),
                  pl.BlockSpec((1, d), lambda i: (0, 0)),
                  mod_spec(which_shift + 1), mod_spec(which_shift)],
        out_specs=pl.BlockSpec((tm, d), lambda i: (i, 0)),
        out_shape=jax.ShapeDtypeStruct((m, d), BF16),
        compiler_params=_cparams(("parallel",), 48),
        name="norm_mod",
    )(x, g.reshape(1, d), mods3, mods3)


def _final_norm_kernel(x_ref, g_ref, o_ref):
    x = x_ref[...]
    y = x * lax.rsqrt(jnp.mean(x * x, axis=-1, keepdims=True) + EPS)
    o_ref[...] = y * g_ref[...]


def _final_norm(x, g, row0, nrows):
    d = x.shape[1]
    tm = _pick(math.gcd(row0, nrows) if row0 else nrows, 512)
    blk0 = row0 // tm
    return pl.pallas_call(
        _final_norm_kernel,
        grid=(nrows // tm,),
        in_specs=[pl.BlockSpec((tm, d), lambda i: (blk0 + i, 0)),
                  pl.BlockSpec((1, d), lambda i: (0, 0))],
        out_specs=pl.BlockSpec((tm, d), lambda i: (i, 0)),
        out_shape=jax.ShapeDtypeStruct((nrows, d), F32),
        compiler_params=_cparams(("parallel",), 48),
        name="final_norm",
    )(x, g.reshape(1, d))


def _mm_kernel(*refs, nk, epilogue):
    if epilogue == "resid":
        a_ref, b_ref, x_ref, gate_ref, o_ref = refs[:5]
        scratch = refs[5:]
    else:
        a_ref, b_ref, o_ref = refs[:3]
        scratch = refs[3:]

    def finish(acc):
        if epilogue == "relu2":
            r = jnp.maximum(acc, 0.0)
            o_ref[...] = (r * r).astype(o_ref.dtype)
        elif epilogue == "resid":
            o_ref[...] = x_ref[...] + gate_ref[...] * acc
        else:
            o_ref[...] = acc.astype(o_ref.dtype)

    def product():
        b = b_ref[...]
        if b.dtype != BF16:
            b = b.astype(BF16)
        return jnp.dot(a_ref[...], b, preferred_element_type=F32)

    if nk == 1:
        finish(product())
        return
    acc_ref = o_ref if epilogue == "resid" else scratch[0]
    k = pl.program_id(2)

    @pl.when(k == 0)
    def _():
        acc_ref[...] = product()

    @pl.when((k > 0) & (k < nk - 1))
    def _():
        acc_ref[...] = acc_ref[...] + product()

    @pl.when(k == nk - 1)
    def _():
        finish(acc_ref[...] + product())


def _matmul(a, b, *, out_dtype, epilogue="plain", resid=None, w_layer=0,
            tm=1024, tn=1024, tk=4096):
    planes = a.shape[0] if a.ndim == 3 else 1
    m, kdim = a.shape[-2:]
    n = b.shape[2]
    tm, tn, tk = _pick(m, tm), _pick(n, tn), _pick(kdim, tk)
    if resid is not None:
        tm = _pick(math.gcd(resid[2].n_p, resid[2].t_l), tm)
    kp = kdim // tk
    nk = planes * kp
    if a.ndim == 3:
        in_specs = [pl.BlockSpec((None, tm, tk), lambda i, j, k: (k // kp, i, k % kp)),
                    pl.BlockSpec((None, tk, tn),
                                 lambda i, j, k: (w_layer * planes + k // kp, k % kp, j))]
    else:
        a_mode = dict(pipeline_mode=pl.Buffered(1)) if (nk == 1 and b.dtype != BF16) else {}
        in_specs = [pl.BlockSpec((tm, tk), lambda i, j, k: (i, k), **a_mode),
                    pl.BlockSpec((None, tk, tn), lambda i, j, k: (w_layer, k, j))]
    args = [a, b]
    if epilogue == "resid":
        x, mods3, rows, layer, which = resid
        in_specs += [pl.BlockSpec((tm, tn), lambda i, j, k: (i, j)),
                     pl.BlockSpec((None, 1, tn),
                                  lambda i, j, k: (rows.mod_row(layer, which, i * tm), 0, j))]
        args += [x, mods3]
    out_bytes = jnp.dtype(out_dtype).itemsize
    b_bytes = jnp.dtype(b.dtype).itemsize
    vmem = 2 * (tm * tk * 2 + tk * tn * b_bytes) + 2 * tm * tn * out_bytes
    if b.dtype != BF16:
        vmem += tk * tn * 2 - (tm * tk * 2 if nk == 1 else 0)
    if epilogue == "resid":
        vmem += 2 * tm * tn * 4
    use_acc = nk > 1 and epilogue != "resid"
    if use_acc:
        vmem += tm * tn * 4
    vmem += 2 * tm * tn * 4
    return pl.pallas_call(
        functools.partial(_mm_kernel, nk=nk, epilogue=epilogue),
        grid=(m // tm, n // tn, nk),
        in_specs=in_specs,
        out_specs=pl.BlockSpec((tm, tn), lambda i, j, k: (i, j)),
        out_shape=jax.ShapeDtypeStruct((m, n), out_dtype),
        scratch_shapes=[pltpu.VMEM((tm, tn), F32)] if use_acc else [],
        compiler_params=_cparams(("parallel", "parallel", "arbitrary"),
                                 min(60, vmem / 2 ** 20 + 6)),
        name="matmul_" + epilogue,
    )(*args)


def _gla_tables(c):
    t = np.arange(c)[:, None]
    r = np.arange(c)[None, :]
    a_f = [(r <= t)]
    a_b = [(r >= t)]
    for lev in range(1, GLA_TABLE_LEVELS + 1):
        p, hh = 1 << lev, 1 << (lev - 1)
        mid = (t // p) * p + hh
        left = (t % p) < hh
        a_f.append(np.where(left, (r > t) & (r <= mid - 1), (r >= mid) & (r <= t)))
        a_b.append(np.where(left, (r >= t) & (r < mid), (r >= mid) & (r < t)))
    a_all = np.stack([np.concatenate(a_f, 0), np.concatenate(a_b, 0)]).astype(np.float32)
    a_all = np.tile(a_all, (1, 1, 2))
    x = np.bitwise_xor(t, r)
    lvl = np.where(x == 0, 0, np.floor(np.log2(np.maximum(x, 1))).astype(np.int64) + 1)
    lv = np.stack([np.where(r <= t, lvl, -1), np.where(r >= t, lvl, -1)]).astype(np.int32)
    return jnp.asarray(a_all, BF16), jnp.asarray(lv)


def _split2(x):
    hi = x.astype(BF16)
    return hi, (x - hi.astype(F32)).astype(BF16)


def _gate_kernel(h_ref, wa1_ref, w2_ref, ba_ref, g1_ref, g2_ref, *, slot):
    a1 = jnp.dot(h_ref[...], wa1_ref[...], preferred_element_type=F32)
    hi = a1.astype(BF16).astype(F32)
    a_cat = (hi + pltpu.roll(a1 - hi, slot, 1) + pltpu.roll(hi, 2 * slot, 1)).astype(BF16)
    n = g1_ref.shape[1]
    tn = _pick(n, 512)
    for j in range(n // tn):
        cols = slice(j * tn, (j + 1) * tn)
        y = jnp.dot(a_cat, w2_ref[:, cols], preferred_element_type=F32) + ba_ref[:, cols]
        g = ((jnp.minimum(y, 0.0) - jnp.log2(1.0 + jnp.exp2(-jnp.abs(y))))
             * (1.0 / GATE_NORM))
        g1_ref[:, cols], g2_ref[:, cols] = _split2(g)


def _log_gates(hb, wa1, w_layer, wa2, ba):
    m, d = hb.shape
    slot, n = wa2.shape
    r = wa1.shape[2]
    assert 3 * slot <= r
    tm = _pick(m, 256)
    w2hi, w2lo = _split2(wa2 * LOG2_E)
    w2 = jnp.zeros((r, n), BF16).at[:3 * slot].set(jnp.concatenate([w2hi, w2hi, w2lo]))
    ba = ba * LOG2_E
    full = lambda shape: pl.BlockSpec(shape, lambda i: (0,) * len(shape))
    return pl.pallas_call(
        functools.partial(_gate_kernel, slot=slot),
        grid=(m // tm,),
        in_specs=[pl.BlockSpec((tm, d), lambda i: (i, 0)),
                  pl.BlockSpec((None, d, r), lambda i: (w_layer, 0, 0)),
                  full((r, n)), full((1, n))],
        out_specs=[pl.BlockSpec((tm, n), lambda i: (i, 0))] * 2,
        out_shape=[jax.ShapeDtypeStruct((m, n), BF16)] * 2,
        compiler_params=_cparams(("parallel",), 48),
        name="log_gates",
    )(hb, wa1, w2, ba)


def _dot_nt(a, b):
    return lax.dot_general(a, b, (((1,), (1,)), ((), ())), preferred_element_type=F32)


def _gla_chunks(chains, a_all_ref, lv_ref, *, c, hk):
    nlev = c.bit_length() - 1
    n = range(len(chains))
    dirs = [ch[0] for ch in chains]
    edge_row = {0: c - 1, 1: 0}
    ee = [jnp.dot(a_all_ref[dirs[i]], jnp.concatenate(chains[i][4](), axis=0),
                  preferred_element_type=F32) for i in n]
    q = lambda i: chains[i][1]()
    k = lambda i: chains[i][2]()
    scale = hk ** -0.5
    qb = [q(i).astype(BF16) for i in n]
    kb = [k(i).astype(BF16) for i in n]
    att = [jnp.where(lv_ref[dirs[i]] == 0, _dot_nt(qb[i], kb[i]), 0.0) for i in n]
    def level_exponent(i, lev):
        if lev <= GLA_TABLE_LEVELS:
            return ee[i][lev * c:(lev + 1) * c]
        b = ee[i][:c]
        p, ref0 = 1 << lev, (1 << (lev - 1)) - 1 + dirs[i]
        ref = [jnp.broadcast_to(b[p0 + ref0:p0 + ref0 + 1], (p, hk)) for p0 in range(0, c, p)]
        return -jnp.abs(b - (ref[0] if len(ref) == 1 else jnp.concatenate(ref, axis=0)))

    for lev in range(1, nlev + 1):
        for i in n:
            u = jnp.exp2(level_exponent(i, lev)).astype(BF16)
            att[i] = att[i] + jnp.where(lv_ref[dirs[i]] == lev,
                                        _dot_nt(qb[i] * u, kb[i] * u), 0.0)
    for i in n:
        b = ee[i][:c]
        vb = chains[i][3]().astype(BF16)
        st = chains[i][5][...]
        o = jnp.dot((q(i) * jnp.exp2(b)).astype(BF16), st.astype(BF16), preferred_element_type=F32)
        o = o + jnp.dot(att[i].astype(BF16), vb, preferred_element_type=F32)
        chains[i][6](o * scale)
        er = edge_row[dirs[i]]
        edge = b[er:er + 1]
        kt = jnp.transpose(k(i) * jnp.exp2(edge - b)).astype(BF16)
        dcol = jnp.transpose(jnp.broadcast_to(jnp.exp2(edge), (128, hk)))[:, :1]
        chains[i][5][...] = dcol * st + jnp.dot(kt, vb, preferred_element_type=F32)


def _gla_kernel(*refs, c, hk, hpb, has_init, emit_final):
    (qf, kf, vf, g1f, g2f, qb, kb, vb, g1b, g2b, aall, lv) = refs[:12]
    pos = 12
    if has_init:
        s0f, s0b = refs[pos:pos + 2]
        pos += 2
    of, ob = refs[pos:pos + 2]
    pos += 2
    if emit_final:
        sff, sfb = refs[pos:pos + 2]
        pos += 2
    sf, sb = refs[pos:pos + 2]
    step = pl.program_id(2)

    @pl.when(step == 0)
    def _():
        if has_init:
            sf[...] = s0f[...]
            sb[...] = s0b[...]
        else:
            sf[...] = jnp.zeros_like(sf)
            sb[...] = jnp.zeros_like(sb)

    hv = of.shape[1] // hpb

    chains = []
    for d, (q, k, v, g1, g2, s, o) in enumerate(((qf, kf, vf, g1f, g2f, sf, of),
                                                 (qb, kb, vb, g1b, g2b, sb, ob))):
        for hh in range(hpb):
            ksl = slice(hh * hk, (hh + 1) * hk)
            vsl = slice(hh * hv, (hh + 1) * hv)

            def store(val, o=o, vsl=vsl):
                o[:, vsl] = val.astype(o.dtype)

            chains.append((d,
                           lambda q=q, ksl=ksl: q[:, ksl],
                           lambda k=k, ksl=ksl: k[:, ksl],
                           lambda v=v, vsl=vsl: v[:, vsl],
                           lambda g1=g1, g2=g2, ksl=ksl: (g1[:, ksl], g2[:, ksl]),
                           s.at[hh], store))
    _gla_chunks(chains, aall, lv, c=c, hk=hk)

    if emit_final:
        @pl.when(step == pl.num_programs(2) - 1)
        def _():
            sff[...] = sf[...]
            sfb[...] = sb[...]


def _gla_scan(z, g1, g2, tables, *, row0, nseq, t, hk, hv,
              layer, n_layers, init=None, final=None, emit_final=False, prev=None):
    m = z.shape[0]
    h = GLA_HEADS
    hpb = GLA_HEADS_PER_STEP
    c = GLA_CHUNK if t % GLA_CHUNK == 0 else t
    nchunk = t // c
    rb0 = row0 // c
    a_all, lv = tables
    fwd = lambda b_, c_: rb0 + b_ * nchunk + c_
    bwd = lambda b_, c_: rb0 + b_ * nchunk + (nchunk - 1 - c_)
    k_col0 = h // hpb
    v_col0 = (2 * h * hk) // (hpb * hv)

    def stream(rowfn, d):
        gate = pl.BlockSpec((c, hpb * hk), lambda b_, h_, c_: (rowfn(b_, c_), d * k_col0 + h_))
        return [pl.BlockSpec((c, hpb * hk), lambda b_, h_, c_: (rowfn(b_, c_), h_)),
                pl.BlockSpec((c, hpb * hk), lambda b_, h_, c_: (rowfn(b_, c_), k_col0 + h_)),
                pl.BlockSpec((c, hpb * hv), lambda b_, h_, c_: (rowfn(b_, c_), v_col0 + h_)),
                gate, gate]

    in_specs = (stream(fwd, 0) + stream(bwd, 1)
                + [pl.BlockSpec(a_all.shape, lambda b_, h_, c_: (0, 0, 0)),
                   pl.BlockSpec(lv.shape, lambda b_, h_, c_: (0, 0, 0))])
    args = [z, z, z, g1, g2, z, z, z, g1, g2, a_all, lv]
    state_spec = pl.BlockSpec((None, None, hpb, hk, hv),
                              lambda b_, h_, c_: (b_, layer, h_, 0, 0))
    if init is not None:
        in_specs += [state_spec, state_spec]
        args += list(init)
    out_specs = [pl.BlockSpec((c, hpb * hv), lambda b_, h_, c_: (fwd(b_, c_), h_)),
                 pl.BlockSpec((c, hpb * hv), lambda b_, h_, c_: (bwd(b_, c_), h_))]
    out_shape = [jax.ShapeDtypeStruct((m, h * hv), BF16)] * 2
    if emit_final:
        out_specs += [state_spec, state_spec]
        out_shape += [jax.ShapeDtypeStruct((nseq, n_layers, h, hk, hv), F32)] * 2
    aliases = {}
    n_blocked = len(args)
    for bufs, out0 in ((prev, 0), (final, 2)):
        if bufs is not None:
            aliases.update({len(args): out0, len(args) + 1: out0 + 1})
            in_specs += [pl.BlockSpec(memory_space=pl.ANY)] * 2
            args += list(bufs)

    def body(*refs):
        refs = refs[:n_blocked] + refs[len(args):]
        _gla_kernel(*refs, c=c, hk=hk, hpb=hpb, has_init=init is not None,
                    emit_final=emit_final)

    return pl.pallas_call(
        body,
        grid=(nseq, h // hpb, nchunk),
        in_specs=in_specs,
        out_specs=out_specs,
        out_shape=out_shape,
        scratch_shapes=[pltpu.VMEM((hpb, hk, hv), F32)] * 2,
        input_output_aliases=aliases,
        compiler_params=_cparams(("parallel", "parallel", "arbitrary"), 56),
        name="gla_scan",
    )(*args)


def _gla_out_kernel(of_ref, ob_ref, r_ref, g_ref, o_ref, *, hv):
    g = g_ref[...]
    for hd in range(GLA_HEADS):
        sl = slice(hd * hv, (hd + 1) * hv)
        o = of_ref[:, sl].astype(F32) + ob_ref[:, sl].astype(F32)
        o = o * lax.rsqrt(jnp.mean(o * o, axis=-1, keepdims=True) + EPS) * g
        r = r_ref[:, sl]
        o_ref[:, sl] = (o * (r * jax.nn.sigmoid(r))).astype(o_ref.dtype)


def _gla_out(o_f, o_b, z, on_g, hv):
    m, d = o_f.shape
    tm = _pick(m, 256)
    r_col = (z.shape[1] - d) // d
    return pl.pallas_call(
        functools.partial(_gla_out_kernel, hv=hv),
        grid=(m // tm,),
        in_specs=[pl.BlockSpec((tm, d), lambda i: (i, 0)),
                  pl.BlockSpec((tm, d), lambda i: (i, 0)),
                  pl.BlockSpec((tm, d), lambda i: (i, r_col)),
                  pl.BlockSpec((1, hv), lambda i: (0, 0))],
        out_specs=pl.BlockSpec((tm, d), lambda i: (i, 0)),
        out_shape=jax.ShapeDtypeStruct((m, d), BF16),
        compiler_params=_cparams(("parallel",), 48),
        name="gla_out",
    )(o_f, o_b, z, on_g.reshape(1, hv))


def _dft_tables(n):
    idx = jnp.arange(n, dtype=jnp.int32)
    ang = ((idx[:, None] * idx[None, :]) % n).astype(F32) * (2.0 * math.pi / n)
    return jnp.stack([jnp.cos(ang), jnp.sin(ang)]).astype(BF16)


def _chan_table(cg):
    half = cg // 2
    cs = _dft_tables(cg)[:, :, :half]
    alt = jnp.where(jnp.arange(cg) % 2 == 0, 1.0, -1.0).astype(BF16)
    nyq = jnp.zeros((cg, NYQ_PAD), BF16).at[:, 0].set(alt)
    return jnp.concatenate([cs[0], cs[1], nyq], axis=1)


def _mirror_rows(d, cg):
    half = cg // 2
    base = (np.arange(d // cg) * cg)[:, None]
    f = np.arange(half)[None, :]
    mirror = np.where(f == 0, half, (cg - f) % cg)
    return np.concatenate([(base + f).reshape(-1), (base + mirror).reshape(-1)])


def _chan_dft_kernel(a_ref, w_ref, o_ref, n_ref, *, half):
    r = jnp.dot(a_ref[...], w_ref[...], preferred_element_type=F32)
    o_ref[0] = r[:, :half].astype(o_ref.dtype)
    o_ref[1] = r[:, half:2 * half].astype(o_ref.dtype)
    n_ref[...] = r[:, 2 * half:].astype(n_ref.dtype)


def _chan_dft(hb, table):
    m, d = hb.shape
    cg = table.shape[0]
    half, groups = cg // 2, d // cg
    tm = _pick(m, 1024)
    return pl.pallas_call(
        functools.partial(_chan_dft_kernel, half=half),
        grid=(m // tm, groups),
        in_specs=[pl.BlockSpec((tm, cg), lambda i, g: (i, g)),
                  pl.BlockSpec(table.shape, lambda i, g: (0, 0))],
        out_specs=[pl.BlockSpec((2, tm, half), lambda i, g: (0, i, g)),
                   pl.BlockSpec((tm, NYQ_PAD), lambda i, g: (i, g))],
        out_shape=[jax.ShapeDtypeStruct((2, m, groups * half), BF16),
                   jax.ShapeDtypeStruct((m, groups * NYQ_PAD), BF16)],
        compiler_params=_cparams(("parallel", "arbitrary"), 40),
        name="chan_dft",
    )(hb, table)


def _seq_dft_whole_kernel(w_ref, p0_ref, p1_ref, xn_ref, o_ref, *, half, scale):
    cp = jnp.dot(w_ref[0], p0_ref[...], preferred_element_type=F32)
    sp = jnp.dot(w_ref[1], p1_ref[...], preferred_element_type=F32)
    cn = jnp.dot(w_ref[0], xn_ref[...], preferred_element_type=F32)
    o_ref[0] = ((cp - sp) * scale).astype(o_ref.dtype)
    o_ref[1] = ((cp + sp) * scale).astype(o_ref.dtype)
    for g in range(o_ref.shape[2] // half):
        o_ref[1, :, g * half:g * half + 1] = (
            cn[:, g * NYQ_PAD:g * NYQ_PAD + 1] * scale).astype(o_ref.dtype)


def _seq_dft_whole(planes, nyq, wt, *, row0, nseq, t, scale, half):
    _, m, n = planes.shape
    assert row0 % t == 0, "sequences must start on a multiple of their length"
    blk0 = row0 // t
    return pl.pallas_call(
        functools.partial(_seq_dft_whole_kernel, half=half, scale=scale),
        grid=(nseq,),
        in_specs=[pl.BlockSpec(wt.shape, lambda b: (0, 0, 0)),
                  pl.BlockSpec((None, t, n), lambda b: (0, blk0 + b, 0)),
                  pl.BlockSpec((None, t, n), lambda b: (1, blk0 + b, 0)),
                  pl.BlockSpec((t, nyq.shape[1]), lambda b: (blk0 + b, 0))],
        out_specs=pl.BlockSpec((2, t, n), lambda b: (0, blk0 + b, 0)),
        out_shape=jax.ShapeDtypeStruct((2, m, n), BF16),
        compiler_params=_cparams(("parallel",), 40),
        name="seq_dft_whole",
    )(wt, planes, planes, nyq)


def _seq_dft_tiled_kernel(w_ref, p_ref, xn_ref, o_ref, cp_ref, cn_ref, *, scale):
    k, g = pl.program_id(2), pl.program_id(3)

    @pl.when(k == 0)
    def _():
        cp_ref[g] = jnp.dot(w_ref[...], p_ref[...], preferred_element_type=F32)
        cn_ref[g] = jnp.dot(w_ref[...], xn_ref[...], preferred_element_type=F32)

    @pl.when(k == 1)
    def _():
        sp = jnp.dot(w_ref[...], p_ref[...], preferred_element_type=F32)
        cp = cp_ref[g]
        o_ref[0] = ((cp - sp) * scale).astype(o_ref.dtype)
        o_ref[1] = ((cp + sp) * scale).astype(o_ref.dtype)
        o_ref[1, :, 0:1] = (cn_ref[g][:, 0:1] * scale).astype(o_ref.dtype)


def _seq_dft_tiled(planes, nyq, wt, y_prev, *, row0, nseq, t, scale, half):
    _, m, n = planes.shape
    groups = n // half
    tm = _pick(t, 1024)
    assert row0 % t == 0, "sequences must start on a multiple of their length"
    blk_t = lambda b: (row0 + b * t) // t
    blk_m = lambda b, i: (row0 + b * t) // tm + i
    return pl.pallas_call(
        lambda w, p, xn, y_any, o, cp, cn: _seq_dft_tiled_kernel(w, p, xn, o, cp, cn, scale=scale),
        grid=(nseq, t // tm, 2, groups),
        in_specs=[pl.BlockSpec((None, tm, t), lambda b, i, k, g: (k, i, 0)),
                  pl.BlockSpec((None, t, half), lambda b, i, k, g: (k, blk_t(b), g)),
                  pl.BlockSpec((t, NYQ_PAD), lambda b, i, k, g: (blk_t(b), g)),
                  pl.BlockSpec(memory_space=pl.ANY)],
        out_specs=pl.BlockSpec((2, tm, half), lambda b, i, k, g: (0, blk_m(b, i), g * k)),
        out_shape=jax.ShapeDtypeStruct((2, m, n), BF16),
        scratch_shapes=[pltpu.VMEM((groups, tm, half), F32),
                        pltpu.VMEM((groups, tm, NYQ_PAD), F32)],
        input_output_aliases={3: 0},
        compiler_params=_cparams(("parallel", "parallel", "arbitrary", "arbitrary"), 56),
        name="seq_dft_tiled",
    )(wt, planes, nyq, y_prev)


def _pos_embed_2d(n_tok, d):
    rows = n_tok // GRID_W
    quarter = d // 4
    omega = 1.0 / (10000.0 ** (jnp.arange(quarter, dtype=F32) / quarter))
    er = jnp.arange(rows, dtype=F32)[:, None] * omega
    ec = jnp.arange(GRID_W, dtype=F32)[:, None] * omega
    er = jnp.concatenate([jnp.sin(er), jnp.cos(er)], axis=-1)
    ec = jnp.concatenate([jnp.sin(ec), jnp.cos(ec)], axis=-1)
    half = d // 2
    emb = jnp.concatenate([jnp.broadcast_to(er[:, None, :], (rows, GRID_W, half)),
                           jnp.broadcast_to(ec[None, :, :], (rows, GRID_W, half))], axis=-1)
    return emb.reshape(rows * GRID_W, d)


def kernel(x_prompt, x_sample, state_fwd, state_bwd, c, c_ctx, norm1_g, norm2_g, w_mod, b_mod,
           gla_w_in, gla_wa1_f, gla_wa2_f, gla_ba_f, gla_wa1_b, gla_wa2_b, gla_ba_b,
           gla_onorm_g, gla_w_out, fnet_w, mlp_w1, mlp_w2, final_g):
    batch, seq, d = x_prompt.shape
    dec_batch, dec_seq, _ = x_sample.shape
    depth = w_mod.shape[0]
    n_p, n_l = batch * seq, dec_batch * dec_seq
    h = GLA_HEADS
    hk, hv = gla_wa2_f.shape[2] // h, gla_onorm_g.shape[1]
    rank = gla_wa1_f.shape[2]
    cg = d // F_GROUPS

    n_cond = -(-(1 + dec_batch) // 8) * 8
    cond = jnp.zeros((n_cond, d), F32).at[0].set(c_ctx).at[1:1 + dec_batch].set(c)
    mods = _modulation(cond, w_mod, b_mod)
    mods3 = mods.reshape(depth * n_cond * N_MOD, 1, d)
    rows = _Rows(n_p, dec_seq, n_cond)

    x, hb0 = _embed(x_prompt.reshape(n_p, d), x_sample.reshape(n_l, d), _pos_embed_2d(dec_seq, d),
                    norm1_g[0], mods3, rows)

    tables = _gla_tables(GLA_CHUNK)
    dft_c = _chan_table(cg)
    dft_tp = _dft_tables(seq)
    dft_tl = _dft_tables(dec_seq)
    w_in, w_out = gla_w_in.astype(BF16), gla_w_out.astype(BF16)
    w_fnet = jnp.take(fnet_w.astype(BF16), _mirror_rows(d, cg), axis=1)
    w_fnet = w_fnet.reshape(2 * fnet_w.shape[0], d // 2, d)
    w2 = mlp_w2.astype(BF16)
    n_gla = gla_w_in.shape[0]
    wa1 = jnp.zeros((n_gla, d, A1_WIDTH), F32)
    wa1 = wa1.at[:, :, :rank].set(gla_wa1_f).at[:, :, rank:2 * rank].set(gla_wa1_b).astype(BF16)
    new_states = None
    for i in range(depth):
        j = i // 2
        hb = hb0 if i == 0 else _norm_mod(x, norm1_g[i], mods3, rows, i, 0)
        resid = lambda which: (x, mods3, rows, i, which)
        if i % 2 == 0:
            z = _matmul(hb, w_in, w_layer=j, out_dtype=F32)
            wa2 = jnp.zeros((2 * rank, 2 * h * hk), F32)
            wa2 = wa2.at[:rank, :h * hk].set(gla_wa2_f[j])
            wa2 = wa2.at[rank:2 * rank, h * hk:].set(gla_wa2_b[j])
            ba = jnp.concatenate([gla_ba_f[j], gla_ba_b[j]]).reshape(1, -1)
            g1, g2 = _log_gates(hb, wa1, j, wa2, ba)
            common = dict(hk=hk, hv=hv, layer=j, n_layers=n_gla)
            o_f, o_b, s_f, s_b = _gla_scan(z, g1, g2, tables, row0=0, nseq=batch, t=seq,
                                           emit_final=True, final=new_states, **common)
            new_states = (s_f, s_b)
            o_f, o_b = _gla_scan(z, g1, g2, tables, row0=n_p, nseq=dec_batch,
                                 t=dec_seq, prev=(o_f, o_b), init=(state_fwd, state_bwd),
                                 **common)
            y = _gla_out(o_f, o_b, z, gla_onorm_g[j], hv)
            x = _matmul(y, w_out, w_layer=j, out_dtype=F32, epilogue="resid", resid=resid(2))
        else:
            planes, nyq = _chan_dft(hb, dft_c)
            y = _seq_dft_whole(planes, nyq, dft_tp, row0=0, nseq=batch, t=seq, half=cg // 2,
                               scale=1.0 / math.sqrt(seq * cg))
            y = _seq_dft_tiled(planes, nyq, dft_tl, y, row0=n_p, nseq=dec_batch, t=dec_seq,
                               half=cg // 2, scale=1.0 / math.sqrt(dec_seq * cg))
            x = _matmul(y, w_fnet, w_layer=j, out_dtype=F32, epilogue="resid", resid=resid(2))
        hb = _norm_mod(x, norm2_g[i], mods3, rows, i, 3)
        a = _matmul(hb, mlp_w1, w_layer=i, out_dtype=BF16, epilogue="relu2", tm=2048, tn=512)
        x = _matmul(a, w2, w_layer=i, out_dtype=F32, epilogue="resid", resid=resid(5))
    y_prompt = _final_norm(x, final_g, 0, n_p).reshape(batch, seq, d)
    y_sample = _final_norm(x, final_g, n_p, n_l).reshape(dec_batch, dec_seq, d)
    return (y_prompt, y_sample) + tuple(new_states)
```

```python
import functools
import math

import numpy as np
import jax
import jax.numpy as jnp
from jax import lax
from jax.experimental import pallas as pl
from jax.experimental.pallas import tpu as pltpu

F32 = jnp.float32
BF16 = jnp.bfloat16

EPS = 1e-6
GLA_HEADS = 8
GATE_NORM = 16.0
LOG2_E = 1.4426950408889634
F_GROUPS = 4
GRID_W = 64
N_MOD = 6
GLA_CHUNK = 128
GLA_HEADS_PER_STEP = 4
GLA_TABLE_LEVELS = 3
A1_WIDTH = 128
NYQ_PAD = 128


def _pick(dim, pref):
    t = min(dim, pref)
    while dim % t:
        t //= 2
    return t


def _cparams(semantics, vmem_mib):
    return pltpu.CompilerParams(dimension_semantics=semantics,
                                vmem_limit_bytes=int(vmem_mib) << 20)


class _Rows:
    def __init__(self, n_p, t_l, n_cond):
        self.n_p, self.t_l, self.n_cond = n_p, t_l, n_cond

    def cond(self, row0):
        return jnp.where(row0 < self.n_p, 0, 1 + (row0 - self.n_p) // self.t_l)

    def mod_row(self, layer, which, row0):
        return (layer * self.n_cond + self.cond(row0)) * N_MOD + which


def _mod_kernel(c_ref, w_ref, b_ref, o_ref):
    c = c_ref[...]
    a = (c * jax.nn.sigmoid(c)).astype(BF16)
    o_ref[...] = jnp.dot(a, w_ref[...].astype(BF16), preferred_element_type=F32) + b_ref[...]


def _modulation(cond, w_mod, b_mod):
    depth, d, n = w_mod.shape
    r = cond.shape[0]
    tn = _pick(n, 512)
    return pl.pallas_call(
        _mod_kernel,
        grid=(depth, n // tn),
        in_specs=[pl.BlockSpec((r, d), lambda l, j: (0, 0)),
                  pl.BlockSpec((None, d, tn), lambda l, j: (l, 0, j)),
                  pl.BlockSpec((None, 1, tn), lambda l, j: (l, 0, j))],
        out_specs=pl.BlockSpec((None, r, tn), lambda l, j: (l, 0, j)),
        out_shape=jax.ShapeDtypeStruct((depth, r, n), F32),
        compiler_params=_cparams(("parallel", "parallel"), 40),
        name="modulation",
    )(cond, w_mod, b_mod.reshape(depth, 1, n))


def _norm_mod_rows(x, g_ref, scale_ref, shift_ref):
    y = x * lax.rsqrt(jnp.mean(x * x, axis=-1, keepdims=True) + EPS)
    return (y * g_ref[...]) * (1.0 + scale_ref[...]) + shift_ref[...]


def _embed_kernel(xp_ref, xl_ref, pos_ref, g_ref, scale_ref, shift_ref, x_ref, h_ref, *, n_pblk):
    i = pl.program_id(0)

    @pl.when(i < n_pblk)
    def _():
        x_ref[...] = xp_ref[...]

    @pl.when(i >= n_pblk)
    def _():
        x_ref[...] = xl_ref[...] + pos_ref[...]

    h_ref[...] = _norm_mod_rows(x_ref[...], g_ref, scale_ref, shift_ref).astype(h_ref.dtype)


def _embed(xp2, xl2, pos, g, mods3, rows):
    n_p, d = xp2.shape
    n_l = xl2.shape[0]
    t_l = pos.shape[0]
    tm = _pick(math.gcd(n_p, t_l), 256)
    n_pblk, n_posblk, n_seq = n_p // tm, t_l // tm, n_l // t_l
    lat = lambda i: jnp.maximum(i - n_pblk, 0)
    pos_blk = lambda i: lat(i) // n_seq
    lat_blk = lambda i: (lat(i) % n_seq) * n_posblk + pos_blk(i)
    row_blk = lambda i: jnp.where(i < n_pblk, i, n_pblk + lat_blk(i))
    mod_spec = lambda which: pl.BlockSpec(
        (None, 1, d), lambda i: (rows.mod_row(0, which, row_blk(i) * tm), 0, 0))
    return pl.pallas_call(
        functools.partial(_embed_kernel, n_pblk=n_pblk),
        grid=((n_p + n_l) // tm,),
        in_specs=[pl.BlockSpec((tm, d), lambda i: (jnp.minimum(i, n_pblk - 1), 0)),
                  pl.BlockSpec((tm, d), lambda i: (lat_blk(i), 0)),
                  pl.BlockSpec((tm, d), lambda i: (pos_blk(i), 0)),
                  pl.BlockSpec((1, d), lambda i: (0, 0)),
                  mod_spec(1), mod_spec(0)],
        out_specs=[pl.BlockSpec((tm, d), lambda i: (row_blk(i), 0))] * 2,
        out_shape=[jax.ShapeDtypeStruct((n_p + n_l, d), F32),
                   jax.ShapeDtypeStruct((n_p + n_l, d), BF16)],
        compiler_params=_cparams(("parallel",), 48),
        name="embed",
    )(xp2, xl2, pos, g.reshape(1, d), mods3, mods3)


def _norm_mod_kernel(x_ref, g_ref, scale_ref, shift_ref, o_ref):
    o_ref[...] = _norm_mod_rows(x_ref[...], g_ref, scale_ref, shift_ref).astype(o_ref.dtype)


def _norm_mod(x, g, mods3, rows, layer, which_shift):
    m, d = x.shape
    tm = _pick(math.gcd(rows.n_p, rows.t_l), 512)
    mod_spec = lambda which: pl.BlockSpec(
        (None, 1, d), lambda i: (rows.mod_row(layer, which, i * tm), 0, 0))
    return pl.pallas_call(
        _norm_mod_kernel,
        grid=(m // tm,),
        in_specs=[pl.BlockSpec((tm, d), lambda i: (i, 0)),
                  pl.BlockSpec((1, d), lambda i: (0, 0)),
                  mod_spec(which_shift + 1), mod_spec(which_shift)],
        out_specs=pl.BlockSpec((tm, d), lambda i: (i, 0)),
        out_shape=jax.ShapeDtypeStruct((m, d), BF16),
        compiler_params=_cparams(("parallel",), 48),
        name="norm_mod",
    )(x, g.reshape(1, d), mods3, mods3)


def _final_norm_kernel(x_ref, g_ref, o_ref):
    x = x_ref[...]
    y = x * lax.rsqrt(jnp.mean(x * x, axis=-1, keepdims=True) + EPS)
    o_ref[...] = y * g_ref[...]


def _final_norm(x, g, row0, nrows):
    d = x.shape[1]
    tm = _pick(math.gcd(row0, nrows) if row0 else nrows, 512)
    blk0 = row0 // tm
    return pl.pallas_call(
        _final_norm_kernel,
        grid=(nrows // tm,),
        in_specs=[pl.BlockSpec((tm, d), lambda i: (blk0 + i, 0)),
                  pl.BlockSpec((1, d), lambda i: (0, 0))],
        out_specs=pl.BlockSpec((tm, d), lambda i: (i, 0)),
        out_shape=jax.ShapeDtypeStruct((nrows, d), F32),
        compiler_params=_cparams(("parallel",), 48),
        name="final_norm",
    )(x, g.reshape(1, d))


def _mm_kernel(*refs, nk, epilogue):
    if epilogue == "resid":
        a_ref, b_ref, x_ref, gate_ref, o_ref = refs[:5]
        scratch = refs[5:]
    else:
        a_ref, b_ref, o_ref = refs[:3]
        scratch = refs[3:]

    def finish(acc):
        if epilogue == "relu2":
            r = jnp.maximum(acc, 0.0)
            o_ref[...] = (r * r).astype(o_ref.dtype)
        elif epilogue == "resid":
            o_ref[...] = x_ref[...] + gate_ref[...] * acc
        else:
            o_ref[...] = acc.astype(o_ref.dtype)

    def product():
        b = b_ref[...]
        if b.dtype != BF16:
            b = b.astype(BF16)
        return jnp.dot(a_ref[...], b, preferred_element_type=F32)

    if nk == 1:
        finish(product())
        return
    acc_ref = o_ref if epilogue == "resid" else scratch[0]
    k = pl.program_id(2)

    @pl.when(k == 0)
    def _():
        acc_ref[...] = product()

    @pl.when((k > 0) & (k < nk - 1))
    def _():
        acc_ref[...] = acc_ref[...] + product()

    @pl.when(k == nk - 1)
    def _():
        finish(acc_ref[...] + product())


def _matmul(a, b, *, out_dtype, epilogue="plain", resid=None, w_layer=0,
            tm=1024, tn=1024, tk=4096):
    planes = a.shape[0] if a.ndim == 3 else 1
    m, kdim = a.shape[-2:]
    n = b.shape[2]
    tm, tn, tk = _pick(m, tm), _pick(n, tn), _pick(kdim, tk)
    if resid is not None:
        tm = _pick(math.gcd(resid[2].n_p, resid[2].t_l), tm)
    kp = kdim // tk
    nk = planes * kp
    if a.ndim == 3:
        in_specs = [pl.BlockSpec((None, tm, tk), lambda i, j, k: (k // kp, i, k % kp)),
                    pl.BlockSpec((None, tk, tn),
                                 lambda i, j, k: (w_layer * planes + k // kp, k % kp, j))]
    else:
        a_mode = dict(pipeline_mode=pl.Buffered(1)) if (nk == 1 and b.dtype != BF16) else {}
        in_specs = [pl.BlockSpec((tm, tk), lambda i, j, k: (i, k), **a_mode),
                    pl.BlockSpec((None, tk, tn), lambda i, j, k: (w_layer, k, j))]
    args = [a, b]
    if epilogue == "resid":
        x, mods3, rows, layer, which = resid
        in_specs += [pl.BlockSpec((tm, tn), lambda i, j, k: (i, j)),
                     pl.BlockSpec((None, 1, tn),
                                  lambda i, j, k: (rows.mod_row(layer, which, i * tm), 0, j))]
        args += [x, mods3]
    out_bytes = jnp.dtype(out_dtype).itemsize
    b_bytes = jnp.dtype(b.dtype).itemsize
    vmem = 2 * (tm * tk * 2 + tk * tn * b_bytes) + 2 * tm * tn * out_bytes
    if b.dtype != BF16:
        vmem += tk * tn * 2 - (tm * tk * 2 if nk == 1 else 0)
    if epilogue == "resid":
        vmem += 2 * tm * tn * 4
    use_acc = nk > 1 and epilogue != "resid"
    if use_acc:
        vmem += tm * tn * 4
    vmem += 2 * tm * tn * 4
    return pl.pallas_call(
        functools.partial(_mm_kernel, nk=nk, epilogue=epilogue),
        grid=(m // tm, n // tn, nk),
        in_specs=in_specs,
        out_specs=pl.BlockSpec((tm, tn), lambda i, j, k: (i, j)),
        out_shape=jax.ShapeDtypeStruct((m, n), out_dtype),
        scratch_shapes=[pltpu.VMEM((tm, tn), F32)] if use_acc else [],
        compiler_params=_cparams(("parallel", "parallel", "arbitrary"),
                                 min(60, vmem / 2 ** 20 + 6)),
        name="matmul_" + epilogue,
    )(*args)


def _gla_tables(c):
    t = np.arange(c)[:, None]
    r = np.arange(c)[None, :]
    a_f = [(r <= t)]
    a_b = [(r >= t)]
    for lev in range(1, GLA_TABLE_LEVELS + 1):
        p, hh = 1 << lev, 1 << (lev - 1)
        mid = (t // p) * p + hh
        left = (t % p) < hh
        a_f.append(np.where(left, (r > t) & (r <= mid - 1), (r >= mid) & (r <= t)))
        a_b.append(np.where(left, (r >= t) & (r < mid), (r >= mid) & (r < t)))
    a_all = np.stack([np.concatenate(a_f, 0), np.concatenate(a_b, 0)]).astype(np.float32)
    a_all = np.tile(a_all, (1, 1, 2))
    x = np.bitwise_xor(t, r)
    lvl = np.where(x == 0, 0, np.floor(np.log2(np.maximum(x, 1))).astype(np.int64) + 1)
    lv = np.stack([np.where(r <= t, lvl, -1), np.where(r >= t, lvl, -1)]).astype(np.int32)
    return jnp.asarray(a_all, BF16), jnp.asarray(lv)


def _split2(x):
    hi = x.astype(BF16)
    return hi, (x - hi.astype(F32)).astype(BF16)


def _gate_kernel(h_ref, wa1_ref, w2_ref, ba_ref, g1_ref, g2_ref, *, slot):
    a1 = jnp.dot(h_ref[...], wa1_ref[...], preferred_element_type=F32)
    hi = a1.astype(BF16).astype(F32)
    a_cat = (hi + pltpu.roll(a1 - hi, slot, 1) + pltpu.roll(hi, 2 * slot, 1)).astype(BF16)
    n = g1_ref.shape[1]
    tn = _pick(n, 512)
    for j in range(n // tn):
        cols = slice(j * tn, (j + 1) * tn)
        y = jnp.dot(a_cat, w2_ref[:, cols], preferred_element_type=F32) + ba_ref[:, cols]
        g = ((jnp.minimum(y, 0.0) - jnp.log2(1.0 + jnp.exp2(-jnp.abs(y))))
             * (1.0 / GATE_NORM))
        g1_ref[:, cols], g2_ref[:, cols] = _split2(g)


def _log_gates(hb, wa1, w_layer, wa2, ba):
    m, d = hb.shape
    slot, n = wa2.shape
    r = wa1.shape[2]
    assert 3 * slot <= r
    tm = _pick(m, 256)
    w2hi, w2lo = _split2(wa2 * LOG2_E)
    w2 = jnp.zeros((r, n), BF16).at[:3 * slot].set(jnp.concatenate([w2hi, w2hi, w2lo]))
    ba = ba * LOG2_E
    full = lambda shape: pl.BlockSpec(shape, lambda i: (0,) * len(shape))
    return pl.pallas_call(
        functools.partial(_gate_kernel, slot=slot),
        grid=(m // tm,),
        in_specs=[pl.BlockSpec((tm, d), lambda i: (i, 0)),
                  pl.BlockSpec((None, d, r), lambda i: (w_layer, 0, 0)),
                  full((r, n)), full((1, n))],
        out_specs=[pl.BlockSpec((tm, n), lambda i: (i, 0))] * 2,
        out_shape=[jax.ShapeDtypeStruct((m, n), BF16)] * 2,
        compiler_params=_cparams(("parallel",), 48),
        name="log_gates",
    )(hb, wa1, w2, ba)


def _dot_nt(a, b):
    return lax.dot_general(a, b, (((1,), (1,)), ((), ())), preferred_element_type=F32)


def _gla_chunks(chains, a_all_ref, lv_ref, *, c, hk):
    nlev = c.bit_length() - 1
    n = range(len(chains))
    dirs = [ch[0] for ch in chains]
    edge_row = {0: c - 1, 1: 0}
    ee = [jnp.dot(a_all_ref[dirs[i]], jnp.concatenate(chains[i][4](), axis=0),
                  preferred_element_type=F32) for i in n]
    q = lambda i: chains[i][1]()
    k = lambda i: chains[i][2]()
    scale = hk ** -0.5
    qb = [q(i).astype(BF16) for i in n]
    kb = [k(i).astype(BF16) for i in n]
    att = [jnp.where(lv_ref[dirs[i]] == 0, _dot_nt(qb[i], kb[i]), 0.0) for i in n]
    def level_exponent(i, lev):
        if lev <= GLA_TABLE_LEVELS:
            return ee[i][lev * c:(lev + 1) * c]
        b = ee[i][:c]
        p, ref0 = 1 << lev, (1 << (lev - 1)) - 1 + dirs[i]
        ref = [jnp.broadcast_to(b[p0 + ref0:p0 + ref0 + 1], (p, hk)) for p0 in range(0, c, p)]
        return -jnp.abs(b - (ref[0] if len(ref) == 1 else jnp.concatenate(ref, axis=0)))

    for lev in range(1, nlev + 1):
        for i in n:
            u = jnp.exp2(level_exponent(i, lev)).astype(BF16)
            att[i] = att[i] + jnp.where(lv_ref[dirs[i]] == lev,
                                        _dot_nt(qb[i] * u, kb[i] * u), 0.0)
    for i in n:
        b = ee[i][:c]
        vb = chains[i][3]().astype(BF16)
        st = chains[i][5][...]
        o = jnp.dot((q(i) * jnp.exp2(b)).astype(BF16), st.astype(BF16), preferred_element_type=F32)
        o = o + jnp.dot(att[i].astype(BF16), vb, preferred_element_type=F32)
        chains[i][6](o * scale)
        er = edge_row[dirs[i]]
        edge = b[er:er + 1]
        kt = jnp.transpose(k(i) * jnp.exp2(edge - b)).astype(BF16)
        dcol = jnp.transpose(jnp.broadcast_to(jnp.exp2(edge), (128, hk)))[:, :1]
        chains[i][5][...] = dcol * st + jnp.dot(kt, vb, preferred_element_type=F32)


def _gla_kernel(*refs, c, hk, hpb, has_init, emit_final):
    (qf, kf, vf, g1f, g2f, qb, kb, vb, g1b, g2b, aall, lv) = refs[:12]
    pos = 12
    if has_init:
        s0f, s0b = refs[pos:pos + 2]
        pos += 2
    of, ob = refs[pos:pos + 2]
    pos += 2
    if emit_final:
        sff, sfb = refs[pos:pos + 2]
        pos += 2
    sf, sb = refs[pos:pos + 2]
    step = pl.program_id(2)

    @pl.when(step == 0)
    def _():
        if has_init:
            sf[...] = s0f[...]
            sb[...] = s0b[...]
        else:
            sf[...] = jnp.zeros_like(sf)
            sb[...] = jnp.zeros_like(sb)

    hv = of.shape[1] // hpb

    chains = []
    for d, (q, k, v, g1, g2, s, o) in enumerate(((qf, kf, vf, g1f, g2f, sf, of),
                                                 (qb, kb, vb, g1b, g2b, sb, ob))):
        for hh in range(hpb):
            ksl = slice(hh * hk, (hh + 1) * hk)
            vsl = slice(hh * hv, (hh + 1) * hv)

            def store(val, o=o, vsl=vsl):
                o[:, vsl] = val.astype(o.dtype)

            chains.append((d,
                           lambda q=q, ksl=ksl: q[:, ksl],
                           lambda k=k, ksl=ksl: k[:, ksl],
                           lambda v=v, vsl=vsl: v[:, vsl],
                           lambda g1=g1, g2=g2, ksl=ksl: (g1[:, ksl], g2[:, ksl]),
                           s.at[hh], store))
    _gla_chunks(chains, aall, lv, c=c, hk=hk)

    if emit_final:
        @pl.when(step == pl.num_programs(2) - 1)
        def _():
            sff[...] = sf[...]
            sfb[...] = sb[...]


def _gla_scan(z, g1, g2, tables, *, row0, nseq, t, hk, hv,
              layer, n_layers, init=None, final=None, emit_final=False, prev=None):
    m = z.shape[0]
    h = GLA_HEADS
    hpb = GLA_HEADS_PER_STEP
    c = GLA_CHUNK if t % GLA_CHUNK == 0 else t
    nchunk = t // c
    rb0 = row0 // c
    a_all, lv = tables
    fwd = lambda b_, c_: rb0 + b_ * nchunk + c_
    bwd = lambda b_, c_: rb0 + b_ * nchunk + (nchunk - 1 - c_)
    k_col0 = h // hpb
    v_col0 = (2 * h * hk) // (hpb * hv)

    def stream(rowfn, d):
        gate = pl.BlockSpec((c, hpb * hk), lambda b_, h_, c_: (rowfn(b_, c_), d * k_col0 + h_))
        return [pl.BlockSpec((c, hpb * hk), lambda b_, h_, c_: (rowfn(b_, c_), h_)),
                pl.BlockSpec((c, hpb * hk), lambda b_, h_, c_: (rowfn(b_, c_), k_col0 + h_)),
                pl.BlockSpec((c, hpb * hv), lambda b_, h_, c_: (rowfn(b_, c_), v_col0 + h_)),
                gate, gate]

    in_specs = (stream(fwd, 0) + stream(bwd, 1)
                + [pl.BlockSpec(a_all.shape, lambda b_, h_, c_: (0, 0, 0)),
                   pl.BlockSpec(lv.shape, lambda b_, h_, c_: (0, 0, 0))])
    args = [z, z, z, g1, g2, z, z, z, g1, g2, a_all, lv]
    state_spec = pl.BlockSpec((None, None, hpb, hk, hv),
                              lambda b_, h_, c_: (b_, layer, h_, 0, 0))
    if init is not None:
        in_specs += [state_spec, state_spec]
        args += list(init)
    out_specs = [pl.BlockSpec((c, hpb * hv), lambda b_, h_, c_: (fwd(b_, c_), h_)),
                 pl.BlockSpec((c, hpb * hv), lambda b_, h_, c_: (bwd(b_, c_), h_))]
    out_shape = [jax.ShapeDtypeStruct((m, h * hv), BF16)] * 2
    if emit_final:
        out_specs += [state_spec, state_spec]
        out_shape += [jax.ShapeDtypeStruct((nseq, n_layers, h, hk, hv), F32)] * 2
    aliases = {}
    n_blocked = len(args)
    for bufs, out0 in ((prev, 0), (final, 2)):
        if bufs is not None:
            aliases.update({len(args): out0, len(args) + 1: out0 + 1})
            in_specs += [pl.BlockSpec(memory_space=pl.ANY)] * 2
            args += list(bufs)

    def body(*refs):
        refs = refs[:n_blocked] + refs[len(args):]
        _gla_kernel(*refs, c=c, hk=hk, hpb=hpb, has_init=init is not None,
                    emit_final=emit_final)

    return pl.pallas_call(
        body,
        grid=(nseq, h // hpb, nchunk),
        in_specs=in_specs,
        out_specs=out_specs,
        out_shape=out_shape,
        scratch_shapes=[pltpu.VMEM((hpb, hk, hv), F32)] * 2,
        input_output_aliases=aliases,
        compiler_params=_cparams(("parallel", "parallel", "arbitrary"), 56),
        name="gla_scan",
    )(*args)


def _gla_out_kernel(of_ref, ob_ref, r_ref, g_ref, o_ref, *, hv):
    g = g_ref[...]
    for hd in range(GLA_HEADS):
        sl = slice(hd * hv, (hd + 1) * hv)
        o = of_ref[:, sl].astype(F32) + ob_ref[:, sl].astype(F32)
        o = o * lax.rsqrt(jnp.mean(o * o, axis=-1, keepdims=True) + EPS) * g
        r = r_ref[:, sl]
        o_ref[:, sl] = (o * (r * jax.nn.sigmoid(r))).astype(o_ref.dtype)


def _gla_out(o_f, o_b, z, on_g, hv):
    m, d = o_f.shape
    tm = _pick(m, 256)
    r_col = (z.shape[1] - d) // d
    return pl.pallas_call(
        functools.partial(_gla_out_kernel, hv=hv),
        grid=(m // tm,),
        in_specs=[pl.BlockSpec((tm, d), lambda i: (i, 0)),
                  pl.BlockSpec((tm, d), lambda i: (i, 0)),
                  pl.BlockSpec((tm, d), lambda i: (i, r_col)),
                  pl.BlockSpec((1, hv), lambda i: (0, 0))],
        out_specs=pl.BlockSpec((tm, d), lambda i: (i, 0)),
        out_shape=jax.ShapeDtypeStruct((m, d), BF16),
        compiler_params=_cparams(("parallel",), 48),
        name="gla_out",
    )(o_f, o_b, z, on_g.reshape(1, hv))


def _cos_sin(j, k, n):
    ang = ((j[:, None] * k[None, :]) % n).astype(F32) * (2.0 * math.pi / n)
    return jnp.cos(ang), jnp.sin(ang)


def _dft_tables(n):
    idx = jnp.arange(n, dtype=jnp.int32)
    blk = 64
    if n % blk or n <= 4 * blk:
        return jnp.stack(_cos_sin(idx, idx, n)).astype(BF16)
    ca, sa = _cos_sin(idx, jnp.arange(n // blk, dtype=jnp.int32) * blk, n)
    cb, sb = _cos_sin(idx, jnp.arange(blk, dtype=jnp.int32), n)
    ca, sa, cb, sb = ca[:, :, None], sa[:, :, None], cb[:, None, :], sb[:, None, :]
    cos = (ca * cb - sa * sb).astype(BF16).reshape(n, n)
    sin = (sa * cb + ca * sb).astype(BF16).reshape(n, n)
    return jnp.stack([cos, sin])


def _chan_table(cg):
    half = cg // 2
    cs = _dft_tables(cg)[:, :, :half]
    alt = jnp.where(jnp.arange(cg) % 2 == 0, 1.0, -1.0).astype(BF16)
    nyq = jnp.zeros((cg, NYQ_PAD), BF16).at[:, 0].set(alt)
    return jnp.concatenate([cs[0], cs[1], nyq], axis=1)


def _mirror_weight(w, cg):
    nl, d, n = w.shape
    half = cg // 2
    w4 = w.astype(BF16).reshape(nl, d // cg, cg, n)
    mirror = jnp.concatenate([w4[:, :, half:half + 1], jnp.flip(w4[:, :, half + 1:], axis=2)],
                             axis=2)
    planes = jnp.stack([w4[:, :, :half].reshape(nl, d // 2, n), mirror.reshape(nl, d // 2, n)],
                       axis=1)
    return planes.reshape(2 * nl, d // 2, n)


def _chan_dft_kernel(a_ref, w_ref, o_ref, n_ref, *, half):
    r = jnp.dot(a_ref[...], w_ref[...], preferred_element_type=F32)
    o_ref[0] = r[:, :half].astype(o_ref.dtype)
    o_ref[1] = r[:, half:2 * half].astype(o_ref.dtype)
    n_ref[...] = r[:, 2 * half:].astype(n_ref.dtype)


def _chan_dft(hb, table):
    m, d = hb.shape
    cg = table.shape[0]
    half, groups = cg // 2, d // cg
    tm = _pick(m, 1024)
    return pl.pallas_call(
        functools.partial(_chan_dft_kernel, half=half),
        grid=(m // tm, groups),
        in_specs=[pl.BlockSpec((tm, cg), lambda i, g: (i, g)),
                  pl.BlockSpec(table.shape, lambda i, g: (0, 0))],
        out_specs=[pl.BlockSpec((2, tm, half), lambda i, g: (0, i, g)),
                   pl.BlockSpec((tm, NYQ_PAD), lambda i, g: (i, g))],
        out_shape=[jax.ShapeDtypeStruct((2, m, groups * half), BF16),
                   jax.ShapeDtypeStruct((m, groups * NYQ_PAD), BF16)],
        compiler_params=_cparams(("parallel", "arbitrary"), 40),
        name="chan_dft",
    )(hb, table)


def _seq_dft_whole_kernel(w_ref, p0_ref, p1_ref, xn_ref, o_ref, *, half, scale):
    cp = jnp.dot(w_ref[0], p0_ref[...], preferred_element_type=F32)
    sp = jnp.dot(w_ref[1], p1_ref[...], preferred_element_type=F32)
    cn = jnp.dot(w_ref[0], xn_ref[...], preferred_element_type=F32)
    o_ref[0] = ((cp - sp) * scale).astype(o_ref.dtype)
    o_ref[1] = ((cp + sp) * scale).astype(o_ref.dtype)
    for g in range(o_ref.shape[2] // half):
        o_ref[1, :, g * half:g * half + 1] = (
            cn[:, g * NYQ_PAD:g * NYQ_PAD + 1] * scale).astype(o_ref.dtype)


def _seq_dft_whole(planes, nyq, wt, *, row0, nseq, t, scale, half):
    _, m, n = planes.shape
    assert row0 % t == 0, "sequences must start on a multiple of their length"
    blk0 = row0 // t
    return pl.pallas_call(
        functools.partial(_seq_dft_whole_kernel, half=half, scale=scale),
        grid=(nseq,),
        in_specs=[pl.BlockSpec(wt.shape, lambda b: (0, 0, 0)),
                  pl.BlockSpec((None, t, n), lambda b: (0, blk0 + b, 0)),
                  pl.BlockSpec((None, t, n), lambda b: (1, blk0 + b, 0)),
                  pl.BlockSpec((t, nyq.shape[1]), lambda b: (blk0 + b, 0))],
        out_specs=pl.BlockSpec((2, t, n), lambda b: (0, blk0 + b, 0)),
        out_shape=jax.ShapeDtypeStruct((2, m, n), BF16),
        compiler_params=_cparams(("parallel",), 40),
        name="seq_dft_whole",
    )(wt, planes, planes, nyq)


def _seq_dft_tiled_kernel(w_ref, p_ref, xn_ref, o_ref, cp_ref, cn_ref, *, scale):
    k, g = pl.program_id(2), pl.program_id(3)

    @pl.when(k == 0)
    def _():
        cp_ref[g] = jnp.dot(w_ref[...], p_ref[...], preferred_element_type=F32)
        cn_ref[g] = jnp.dot(w_ref[...], xn_ref[...], preferred_element_type=F32)

    @pl.when(k == 1)
    def _():
        sp = jnp.dot(w_ref[...], p_ref[...], preferred_element_type=F32)
        cp = cp_ref[g]
        o_ref[0] = ((cp - sp) * scale).astype(o_ref.dtype)
        o_ref[1] = ((cp + sp) * scale).astype(o_ref.dtype)
        o_ref[1, :, 0:1] = (cn_ref[g][:, 0:1] * scale).astype(o_ref.dtype)


def _seq_dft_tiled(planes, nyq, wt, y_prev, *, row0, nseq, t, scale, half):
    _, m, n = planes.shape
    groups = n // half
    tm = _pick(t, 1024)
    assert row0 % t == 0, "sequences must start on a multiple of their length"
    blk_t = lambda b: (row0 + b * t) // t
    blk_m = lambda b, i: (row0 + b * t) // tm + i
    return pl.pallas_call(
        lambda w, p, xn, y_any, o, cp, cn: _seq_dft_tiled_kernel(w, p, xn, o, cp, cn, scale=scale),
        grid=(nseq, t // tm, 2, groups),
        in_specs=[pl.BlockSpec((None, tm, t), lambda b, i, k, g: (k, i, 0)),
                  pl.BlockSpec((None, t, half), lambda b, i, k, g: (k, blk_t(b), g)),
                  pl.BlockSpec((t, NYQ_PAD), lambda b, i, k, g: (blk_t(b), g)),
                  pl.BlockSpec(memory_space=pl.ANY)],
        out_specs=pl.BlockSpec((2, tm, half), lambda b, i, k, g: (0, blk_m(b, i), g * k)),
        out_shape=jax.ShapeDtypeStruct((2, m, n), BF16),
        scratch_shapes=[pltpu.VMEM((groups, tm, half), F32),
                        pltpu.VMEM((groups, tm, NYQ_PAD), F32)],
        input_output_aliases={3: 0},
        compiler_params=_cparams(("parallel", "parallel", "arbitrary", "arbitrary"), 56),
        name="seq_dft_tiled",
    )(wt, planes, nyq, y_prev)


def _pos_embed_2d(n_tok, d):
    rows = n_tok // GRID_W
    quarter = d // 4
    omega = 1.0 / (10000.0 ** (jnp.arange(quarter, dtype=F32) / quarter))
    er = jnp.arange(rows, dtype=F32)[:, None] * omega
    ec = jnp.arange(GRID_W, dtype=F32)[:, None] * omega
    er = jnp.concatenate([jnp.sin(er), jnp.cos(er)], axis=-1)
    ec = jnp.concatenate([jnp.sin(ec), jnp.cos(ec)], axis=-1)
    half = d // 2
    emb = jnp.concatenate([jnp.broadcast_to(er[:, None, :], (rows, GRID_W, half)),
                           jnp.broadcast_to(ec[None, :, :], (rows, GRID_W, half))], axis=-1)
    return emb.reshape(rows * GRID_W, d)


def kernel(x_prompt, x_sample, state_fwd, state_bwd, c, c_ctx, norm1_g, norm2_g, w_mod, b_mod,
           gla_w_in, gla_wa1_f, gla_wa2_f, gla_ba_f, gla_wa1_b, gla_wa2_b, gla_ba_b,
           gla_onorm_g, gla_w_out, fnet_w, mlp_w1, mlp_w2, final_g):
    batch, seq, d = x_prompt.shape
    dec_batch, dec_seq, _ = x_sample.shape
    depth = w_mod.shape[0]
    n_p, n_l = batch * seq, dec_batch * dec_seq
    h = GLA_HEADS
    hk, hv = gla_wa2_f.shape[2] // h, gla_onorm_g.shape[1]
    rank = gla_wa1_f.shape[2]
    cg = d // F_GROUPS

    n_cond = -(-(1 + dec_batch) // 8) * 8
    cond = jnp.zeros((n_cond, d), F32).at[0].set(c_ctx).at[1:1 + dec_batch].set(c)
    mods = _modulation(cond, w_mod, b_mod)
    mods3 = mods.reshape(depth * n_cond * N_MOD, 1, d)
    rows = _Rows(n_p, dec_seq, n_cond)

    x, hb0 = _embed(x_prompt.reshape(n_p, d), x_sample.reshape(n_l, d), _pos_embed_2d(dec_seq, d),
                    norm1_g[0], mods3, rows)

    tables = _gla_tables(GLA_CHUNK)
    dft_c = _chan_table(cg)
    dft_tp = _dft_tables(seq)
    dft_tl = _dft_tables(dec_seq)
    w_in, w_out = gla_w_in.astype(BF16), gla_w_out.astype(BF16)
    w_fnet = _mirror_weight(fnet_w, cg)
    w2 = mlp_w2.astype(BF16)
    n_gla = gla_w_in.shape[0]
    wa1 = jnp.zeros((n_gla, d, A1_WIDTH), F32)
    wa1 = wa1.at[:, :, :rank].set(gla_wa1_f).at[:, :, rank:2 * rank].set(gla_wa1_b).astype(BF16)
    new_states = None
    for i in range(depth):
        j = i // 2
        hb = hb0 if i == 0 else _norm_mod(x, norm1_g[i], mods3, rows, i, 0)
        resid = lambda which: (x, mods3, rows, i, which)
        if i % 2 == 0:
            z = _matmul(hb, w_in, w_layer=j, out_dtype=F32)
            wa2 = jnp.zeros((2 * rank, 2 * h * hk), F32)
            wa2 = wa2.at[:rank, :h * hk].set(gla_wa2_f[j])
            wa2 = wa2.at[rank:2 * rank, h * hk:].set(gla_wa2_b[j])
            ba = jnp.concatenate([gla_ba_f[j], gla_ba_b[j]]).reshape(1, -1)
            g1, g2 = _log_gates(hb, wa1, j, wa2, ba)
            common = dict(hk=hk, hv=hv, layer=j, n_layers=n_gla)
            o_f, o_b, s_f, s_b = _gla_scan(z, g1, g2, tables, row0=0, nseq=batch, t=seq,
                                           emit_final=True, final=new_states, **common)
            new_states = (s_f, s_b)
            o_f, o_b = _gla_scan(z, g1, g2, tables, row0=n_p, nseq=dec_batch,
                                 t=dec_seq, prev=(o_f, o_b), init=(state_fwd, state_bwd),
                                 **common)
            y = _gla_out(o_f, o_b, z, gla_onorm_g[j], hv)
            x = _matmul(y, w_out, w_layer=j, out_dtype=F32, epilogue="resid", resid=resid(2))
        else:
            planes, nyq = _chan_dft(hb, dft_c)
            y = _seq_dft_whole(planes, nyq, dft_tp, row0=0, nseq=batch, t=seq, half=cg // 2,
                               scale=1.0 / math.sqrt(seq * cg))
            y = _seq_dft_tiled(planes, nyq, dft_tl, y, row0=n_p, nseq=dec_batch, t=dec_seq,
                               half=cg // 2, scale=1.0 / math.sqrt(dec_seq * cg))
            x = _matmul(y, w_fnet, w_layer=j, out_dtype=F32, epilogue="resid", resid=resid(2))
        hb = _norm_mod(x, norm2_g[i], mods3, rows, i, 3)
        a = _matmul(hb, mlp_w1, w_layer=i, out_dtype=BF16, epilogue="relu2", tm=2048, tn=512)
        x = _matmul(a, w2, w_layer=i, out_dtype=F32, epilogue="resid", resid=resid(5))
    y_prompt = _final_norm(x, final_g, 0, n_p).reshape(batch, seq, d)
    y_sample = _final_norm(x, final_g, n_p, n_l).reshape(dec_batch, dec_seq, d)
    return (y_prompt, y_sample) + tuple(new_states)
```

```python
import functools
import math

import numpy as np
import jax
import jax.numpy as jnp
from jax import lax
from jax.experimental import pallas as pl
from jax.experimental.pallas import tpu as pltpu

F32 = jnp.float32
BF16 = jnp.bfloat16

EPS = 1e-6
GLA_HEADS = 8
GATE_NORM = 16.0
LOG2_E = 1.4426950408889634
F_GROUPS = 4
GRID_W = 64
N_MOD = 6
GLA_CHUNK = 128
GLA_HEADS_PER_STEP = 4
GLA_TABLE_LEVELS = 3
A1_WIDTH = 128
NYQ_PAD = 128


def _pick(dim, pref):
    t = min(dim, pref)
    while dim % t:
        t //= 2
    return t


def _cparams(semantics, vmem_mib):
    return pltpu.CompilerParams(dimension_semantics=semantics,
                                vmem_limit_bytes=int(vmem_mib) << 20)


class _Rows:
    def __init__(self, n_p, t_l, n_cond):
        self.n_p, self.t_l, self.n_cond = n_p, t_l, n_cond

    def cond(self, row0):
        return jnp.where(row0 < self.n_p, 0, 1 + (row0 - self.n_p) // self.t_l)

    def mod_row(self, layer, which, row0):
        return (layer * self.n_cond + self.cond(row0)) * N_MOD + which


def _mod_kernel(c_ref, w_ref, b_ref, o_ref):
    c = c_ref[...]
    a = (c * jax.nn.sigmoid(c)).astype(BF16)
    o_ref[...] = jnp.dot(a, w_ref[...].astype(BF16), preferred_element_type=F32) + b_ref[...]


def _modulation(cond, w_mod, b_mod):
    depth, d, n = w_mod.shape
    r = cond.shape[0]
    tn = _pick(n, 512)
    return pl.pallas_call(
        _mod_kernel,
        grid=(depth, n // tn),
        in_specs=[pl.BlockSpec((r, d), lambda l, j: (0, 0)),
                  pl.BlockSpec((None, d, tn), lambda l, j: (l, 0, j)),
                  pl.BlockSpec((None, 1, tn), lambda l, j: (l, 0, j))],
        out_specs=pl.BlockSpec((None, r, tn), lambda l, j: (l, 0, j)),
        out_shape=jax.ShapeDtypeStruct((depth, r, n), F32),
        compiler_params=_cparams(("parallel", "parallel"), 40),
        name="modulation",
    )(cond, w_mod, b_mod.reshape(depth, 1, n))


def _norm_mod_rows(x, g_ref, scale_ref, shift_ref):
    y = x * lax.rsqrt(jnp.mean(x * x, axis=-1, keepdims=True) + EPS)
    return (y * g_ref[...]) * (1.0 + scale_ref[...]) + shift_ref[...]


def _embed_kernel(xp_ref, xl_ref, pos_ref, g_ref, scale_ref, shift_ref, x_ref, h_ref, *, n_pblk):
    i = pl.program_id(0)

    @pl.when(i < n_pblk)
    def _():
        x_ref[...] = xp_ref[...]

    @pl.when(i >= n_pblk)
    def _():
        x_ref[...] = xl_ref[...] + pos_ref[...]

    h_ref[...] = _norm_mod_rows(x_ref[...], g_ref, scale_ref, shift_ref).astype(h_ref.dtype)


def _embed(xp2, xl2, pos, g, mods3, rows):
    n_p, d = xp2.shape
    n_l = xl2.shape[0]
    t_l = pos.shape[0]
    tm = _pick(math.gcd(n_p, t_l), 256)
    n_pblk, n_posblk, n_seq = n_p // tm, t_l // tm, n_l // t_l
    lat = lambda i: jnp.maximum(i - n_pblk, 0)
    pos_blk = lambda i: lat(i) // n_seq
    lat_blk = lambda i: (lat(i) % n_seq) * n_posblk + pos_blk(i)
    row_blk = lambda i: jnp.where(i < n_pblk, i, n_pblk + lat_blk(i))
    mod_spec = lambda which: pl.BlockSpec(
        (None, 1, d), lambda i: (rows.mod_row(0, which, row_blk(i) * tm), 0, 0))
    return pl.pallas_call(
        functools.partial(_embed_kernel, n_pblk=n_pblk),
        grid=((n_p + n_l) // tm,),
        in_specs=[pl.BlockSpec((tm, d), lambda i: (jnp.minimum(i, n_pblk - 1), 0)),
                  pl.BlockSpec((tm, d), lambda i: (lat_blk(i), 0)),
                  pl.BlockSpec((tm, d), lambda i: (pos_blk(i), 0)),
                  pl.BlockSpec((1, d), lambda i: (0, 0)),
                  mod_spec(1), mod_spec(0)],
        out_specs=[pl.BlockSpec((tm, d), lambda i: (row_blk(i), 0))] * 2,
        out_shape=[jax.ShapeDtypeStruct((n_p + n_l, d), F32),
                   jax.ShapeDtypeStruct((n_p + n_l, d), BF16)],
        compiler_params=_cparams(("parallel",), 48),
        name="embed",
    )(xp2, xl2, pos, g.reshape(1, d), mods3, mods3)


def _norm_mod_kernel(x_ref, g_ref, scale_ref, shift_ref, o_ref):
    o_ref[...] = _norm_mod_rows(x_ref[...], g_ref, scale_ref, shift_ref).astype(o_ref.dtype)


def _norm_mod(x, g, mods3, rows, layer, which_shift):
    m, d = x.shape
    tm = _pick(math.gcd(rows.n_p, rows.t_l), 512)
    mod_spec = lambda which: pl.BlockSpec(
        (None, 1, d), lambda i: (rows.mod_row(layer, which, i * tm), 0, 0))
    return pl.pallas_call(
        _norm_mod_kernel,
        grid=(m // tm,),
        in_specs=[pl.BlockSpec((tm, d), lambda i: (i, 0)),
                  pl.BlockSpec((1, d), lambda i: (0, 0)),
                  mod_spec(which_shift + 1), mod_spec(which_shift)],
        out_specs=pl.BlockSpec((tm, d), lambda i: (i, 0)),
        out_shape=jax.ShapeDtypeStruct((m, d), BF16),
        compiler_params=_cparams(("parallel",), 48),
        name="norm_mod",
    )(x, g.reshape(1, d), mods3, mods3)


def _final_norm_kernel(x_ref, g_ref, o_ref):
    x = x_ref[...]
    y = x * lax.rsqrt(jnp.mean(x * x, axis=-1, keepdims=True) + EPS)
    o_ref[...] = y * g_ref[...]


def _final_norm(x, g, row0, nrows):
    d = x.shape[1]
    tm = _pick(math.gcd(row0, nrows) if row0 else nrows, 512)
    blk0 = row0 // tm
    return pl.pallas_call(
        _final_norm_kernel,
        grid=(nrows // tm,),
        in_specs=[pl.BlockSpec((tm, d), lambda i: (blk0 + i, 0)),
                  pl.BlockSpec((1, d), lambda i: (0, 0))],
        out_specs=pl.BlockSpec((tm, d), lambda i: (i, 0)),
        out_shape=jax.ShapeDtypeStruct((nrows, d), F32),
        compiler_params=_cparams(("parallel",), 48),
        name="final_norm",
    )(x, g.reshape(1, d))


def _mm_kernel(*refs, nk, epilogue):
    if epilogue == "resid":
        a_ref, b_ref, x_ref, gate_ref, o_ref = refs[:5]
        scratch = refs[5:]
    else:
        a_ref, b_ref, o_ref = refs[:3]
        scratch = refs[3:]

    def finish(acc):
        if epilogue == "relu2":
            r = jnp.maximum(acc, 0.0)
            o_ref[...] = (r * r).astype(o_ref.dtype)
        elif epilogue == "resid":
            o_ref[...] = x_ref[...] + gate_ref[...] * acc
        else:
            o_ref[...] = acc.astype(o_ref.dtype)

    def product():
        b = b_ref[...]
        if b.dtype != BF16:
            b = b.astype(BF16)
        return jnp.dot(a_ref[...], b, preferred_element_type=F32)

    if nk == 1:
        finish(product())
        return
    acc_ref = o_ref if epilogue == "resid" else scratch[0]
    k = pl.program_id(2)

    @pl.when(k == 0)
    def _():
        acc_ref[...] = product()

    @pl.when((k > 0) & (k < nk - 1))
    def _():
        acc_ref[...] = acc_ref[...] + product()

    @pl.when(k == nk - 1)
    def _():
        finish(acc_ref[...] + product())


def _matmul(a, b, *, out_dtype, epilogue="plain", resid=None, w_layer=0,
            tm=1024, tn=1024, tk=4096):
    planes = a.shape[0] if a.ndim == 3 else 1
    m, kdim = a.shape[-2:]
    n = b.shape[2]
    tm, tn, tk = _pick(m, tm), _pick(n, tn), _pick(kdim, tk)
    if resid is not None:
        tm = _pick(math.gcd(resid[2].n_p, resid[2].t_l), tm)
    kp = kdim // tk
    nk = planes * kp
    if a.ndim == 3:
        in_specs = [pl.BlockSpec((None, tm, tk), lambda i, j, k: (k // kp, i, k % kp)),
                    pl.BlockSpec((None, tk, tn),
                                 lambda i, j, k: (w_layer * planes + k // kp, k % kp, j))]
    else:
        a_mode = dict(pipeline_mode=pl.Buffered(1)) if (nk == 1 and b.dtype != BF16) else {}
        in_specs = [pl.BlockSpec((tm, tk), lambda i, j, k: (i, k), **a_mode),
                    pl.BlockSpec((None, tk, tn), lambda i, j, k: (w_layer, k, j))]
    args = [a, b]
    if epilogue == "resid":
        x, mods3, rows, layer, which = resid
        in_specs += [pl.BlockSpec((tm, tn), lambda i, j, k: (i, j)),
                     pl.BlockSpec((None, 1, tn),
                                  lambda i, j, k: (rows.mod_row(layer, which, i * tm), 0, j))]
        args += [x, mods3]
    out_bytes = jnp.dtype(out_dtype).itemsize
    b_bytes = jnp.dtype(b.dtype).itemsize
    vmem = 2 * (tm * tk * 2 + tk * tn * b_bytes) + 2 * tm * tn * out_bytes
    if b.dtype != BF16:
        vmem += tk * tn * 2 - (tm * tk * 2 if nk == 1 else 0)
    if epilogue == "resid":
        vmem += 2 * tm * tn * 4
    use_acc = nk > 1 and epilogue != "resid"
    if use_acc:
        vmem += tm * tn * 4
    vmem += 2 * tm * tn * 4
    return pl.pallas_call(
        functools.partial(_mm_kernel, nk=nk, epilogue=epilogue),
        grid=(m // tm, n // tn, nk),
        in_specs=in_specs,
        out_specs=pl.BlockSpec((tm, tn), lambda i, j, k: (i, j)),
        out_shape=jax.ShapeDtypeStruct((m, n), out_dtype),
        scratch_shapes=[pltpu.VMEM((tm, tn), F32)] if use_acc else [],
        compiler_params=_cparams(("parallel", "parallel", "arbitrary"),
                                 min(60, vmem / 2 ** 20 + 6)),
        name="matmul_" + epilogue,
    )(*args)


def _gla_tables(c):
    t = np.arange(c)[:, None]
    r = np.arange(c)[None, :]
    a_f = [(r <= t)]
    a_b = [(r >= t)]
    for lev in range(1, GLA_TABLE_LEVELS + 1):
        p, hh = 1 << lev, 1 << (lev - 1)
        mid = (t // p) * p + hh
        left = (t % p) < hh
        a_f.append(np.where(left, (r > t) & (r <= mid - 1), (r >= mid) & (r <= t)))
        a_b.append(np.where(left, (r >= t) & (r < mid), (r >= mid) & (r < t)))
    a_all = np.stack([np.concatenate(a_f, 0), np.concatenate(a_b, 0)]).astype(np.float32)
    a_all = np.tile(a_all, (1, 1, 2))
    x = np.bitwise_xor(t, r)
    lvl = np.where(x == 0, 0, np.floor(np.log2(np.maximum(x, 1))).astype(np.int64) + 1)
    lv = np.stack([np.where(r <= t, lvl, -1), np.where(r >= t, lvl, -1)]).astype(np.int32)
    return jnp.asarray(a_all, BF16), jnp.asarray(lv)


def _split2(x):
    hi = x.astype(BF16)
    return hi, (x - hi.astype(F32)).astype(BF16)


def _gate_kernel(h_ref, wa1_ref, w2_ref, ba_ref, g1_ref, g2_ref, *, slot):
    a1 = jnp.dot(h_ref[...], wa1_ref[...], preferred_element_type=F32)
    hi = a1.astype(BF16).astype(F32)
    a_cat = (hi + pltpu.roll(a1 - hi, slot, 1) + pltpu.roll(hi, 2 * slot, 1)).astype(BF16)
    n = g1_ref.shape[1]
    tn = _pick(n, 512)
    for j in range(n // tn):
        cols = slice(j * tn, (j + 1) * tn)
        y = jnp.dot(a_cat, w2_ref[:, cols], preferred_element_type=F32) + ba_ref[:, cols]
        g = ((jnp.minimum(y, 0.0) - jnp.log2(1.0 + jnp.exp2(-jnp.abs(y))))
             * (1.0 / GATE_NORM))
        g1_ref[:, cols], g2_ref[:, cols] = _split2(g)


def _log_gates(hb, wa1, w_layer, wa2, ba):
    m, d = hb.shape
    slot, n = wa2.shape
    r = wa1.shape[2]
    assert 3 * slot <= r
    tm = _pick(m, 256)
    w2hi, w2lo = _split2(wa2 * LOG2_E)
    w2 = jnp.zeros((r, n), BF16).at[:3 * slot].set(jnp.concatenate([w2hi, w2hi, w2lo]))
    ba = ba * LOG2_E
    full = lambda shape: pl.BlockSpec(shape, lambda i: (0,) * len(shape))
    return pl.pallas_call(
        functools.partial(_gate_kernel, slot=slot),
        grid=(m // tm,),
        in_specs=[pl.BlockSpec((tm, d), lambda i: (i, 0)),
                  pl.BlockSpec((None, d, r), lambda i: (w_layer, 0, 0)),
                  full((r, n)), full((1, n))],
        out_specs=[pl.BlockSpec((tm, n), lambda i: (i, 0))] * 2,
        out_shape=[jax.ShapeDtypeStruct((m, n), BF16)] * 2,
        compiler_params=_cparams(("parallel",), 48),
        name="log_gates",
    )(hb, wa1, w2, ba)


def _dot_nt(a, b):
    return lax.dot_general(a, b, (((1,), (1,)), ((), ())), preferred_element_type=F32)


def _gla_chunks(chains, a_all_ref, lv_ref, *, c, hk):
    nlev = c.bit_length() - 1
    n = range(len(chains))
    dirs = [ch[0] for ch in chains]
    edge_row = {0: c - 1, 1: 0}
    ee = [jnp.dot(a_all_ref[dirs[i]], jnp.concatenate(chains[i][4](), axis=0),
                  preferred_element_type=F32) for i in n]
    q = lambda i: chains[i][1]()
    k = lambda i: chains[i][2]()
    scale = hk ** -0.5
    qb = [q(i).astype(BF16) for i in n]
    kb = [k(i).astype(BF16) for i in n]
    att = [jnp.where(lv_ref[dirs[i]] == 0, _dot_nt(qb[i], kb[i]), 0.0) for i in n]
    def level_exponent(i, lev):
        if lev <= GLA_TABLE_LEVELS:
            return ee[i][lev * c:(lev + 1) * c]
        b = ee[i][:c]
        p, ref0 = 1 << lev, (1 << (lev - 1)) - 1 + dirs[i]
        ref = [jnp.broadcast_to(b[p0 + ref0:p0 + ref0 + 1], (p, hk)) for p0 in range(0, c, p)]
        return -jnp.abs(b - (ref[0] if len(ref) == 1 else jnp.concatenate(ref, axis=0)))

    for lev in range(1, nlev + 1):
        for i in n:
            u = jnp.exp2(level_exponent(i, lev)).astype(BF16)
            att[i] = att[i] + jnp.where(lv_ref[dirs[i]] == lev,
                                        _dot_nt(qb[i] * u, kb[i] * u), 0.0)
    for i in n:
        b = ee[i][:c]
        vb = chains[i][3]().astype(BF16)
        st = chains[i][5][...]
        o = jnp.dot((q(i) * jnp.exp2(b)).astype(BF16), st.astype(BF16), preferred_element_type=F32)
        o = o + jnp.dot(att[i].astype(BF16), vb, preferred_element_type=F32)
        chains[i][6](o * scale)
        er = edge_row[dirs[i]]
        edge = b[er:er + 1]
        kt = jnp.transpose(k(i) * jnp.exp2(edge - b)).astype(BF16)
        dcol = jnp.transpose(jnp.broadcast_to(jnp.exp2(edge), (128, hk)))[:, :1]
        chains[i][5][...] = dcol * st + jnp.dot(kt, vb, preferred_element_type=F32)


def _gla_kernel(*refs, c, hk, hpb, has_init, emit_final):
    (qf, kf, vf, g1f, g2f, qb, kb, vb, g1b, g2b, aall, lv) = refs[:12]
    pos = 12
    if has_init:
        s0f, s0b = refs[pos:pos + 2]
        pos += 2
    of, ob = refs[pos:pos + 2]
    pos += 2
    if emit_final:
        sff, sfb = refs[pos:pos + 2]
        pos += 2
    sf, sb = refs[pos:pos + 2]
    step = pl.program_id(2)

    @pl.when(step == 0)
    def _():
        if has_init:
            sf[...] = s0f[...]
            sb[...] = s0b[...]
        else:
            sf[...] = jnp.zeros_like(sf)
            sb[...] = jnp.zeros_like(sb)

    hv = of.shape[1] // hpb

    chains = []
    for d, (q, k, v, g1, g2, s, o) in enumerate(((qf, kf, vf, g1f, g2f, sf, of),
                                                 (qb, kb, vb, g1b, g2b, sb, ob))):
        for hh in range(hpb):
            ksl = slice(hh * hk, (hh + 1) * hk)
            vsl = slice(hh * hv, (hh + 1) * hv)

            def store(val, o=o, vsl=vsl):
                o[:, vsl] = val.astype(o.dtype)

            chains.append((d,
                           lambda q=q, ksl=ksl: q[:, ksl],
                           lambda k=k, ksl=ksl: k[:, ksl],
                           lambda v=v, vsl=vsl: v[:, vsl],
                           lambda g1=g1, g2=g2, ksl=ksl: (g1[:, ksl], g2[:, ksl]),
                           s.at[hh], store))
    _gla_chunks(chains, aall, lv, c=c, hk=hk)

    if emit_final:
        @pl.when(step == pl.num_programs(2) - 1)
        def _():
            sff[...] = sf[...]
            sfb[...] = sb[...]


def _gla_scan(z, g1, g2, tables, *, row0, nseq, t, hk, hv,
              layer, n_layers, init=None, final=None, emit_final=False, prev=None):
    m = z.shape[0]
    h = GLA_HEADS
    hpb = GLA_HEADS_PER_STEP
    c = GLA_CHUNK if t % GLA_CHUNK == 0 else t
    nchunk = t // c
    rb0 = row0 // c
    a_all, lv = tables
    fwd = lambda b_, c_: rb0 + b_ * nchunk + c_
    bwd = lambda b_, c_: rb0 + b_ * nchunk + (nchunk - 1 - c_)
    k_col0 = h // hpb
    v_col0 = (2 * h * hk) // (hpb * hv)

    def stream(rowfn, d):
        gate = pl.BlockSpec((c, hpb * hk), lambda b_, h_, c_: (rowfn(b_, c_), d * k_col0 + h_))
        return [pl.BlockSpec((c, hpb * hk), lambda b_, h_, c_: (rowfn(b_, c_), h_)),
                pl.BlockSpec((c, hpb * hk), lambda b_, h_, c_: (rowfn(b_, c_), k_col0 + h_)),
                pl.BlockSpec((c, hpb * hv), lambda b_, h_, c_: (rowfn(b_, c_), v_col0 + h_)),
                gate, gate]

    in_specs = (stream(fwd, 0) + stream(bwd, 1)
                + [pl.BlockSpec(a_all.shape, lambda b_, h_, c_: (0, 0, 0)),
                   pl.BlockSpec(lv.shape, lambda b_, h_, c_: (0, 0, 0))])
    args = [z, z, z, g1, g2, z, z, z, g1, g2, a_all, lv]
    state_spec = pl.BlockSpec((None, None, hpb, hk, hv),
                              lambda b_, h_, c_: (b_, layer, h_, 0, 0))
    if init is not None:
        in_specs += [state_spec, state_spec]
        args += list(init)
    out_specs = [pl.BlockSpec((c, hpb * hv), lambda b_, h_, c_: (fwd(b_, c_), h_)),
                 pl.BlockSpec((c, hpb * hv), lambda b_, h_, c_: (bwd(b_, c_), h_))]
    out_shape = [jax.ShapeDtypeStruct((m, h * hv), BF16)] * 2
    if emit_final:
        out_specs += [state_spec, state_spec]
        out_shape += [jax.ShapeDtypeStruct((nseq, n_layers, h, hk, hv), F32)] * 2
    aliases = {}
    n_blocked = len(args)
    for bufs, out0 in ((prev, 0), (final, 2)):
        if bufs is not None:
            aliases.update({len(args): out0, len(args) + 1: out0 + 1})
            in_specs += [pl.BlockSpec(memory_space=pl.ANY)] * 2
            args += list(bufs)

    def body(*refs):
        refs = refs[:n_blocked] + refs[len(args):]
        _gla_kernel(*refs, c=c, hk=hk, hpb=hpb, has_init=init is not None,
                    emit_final=emit_final)

    return pl.pallas_call(
        body,
        grid=(nseq, h // hpb, nchunk),
        in_specs=in_specs,
        out_specs=out_specs,
        out_shape=out_shape,
        scratch_shapes=[pltpu.VMEM((hpb, hk, hv), F32)] * 2,
        input_output_aliases=aliases,
        compiler_params=_cparams(("parallel", "parallel", "arbitrary"), 56),
        name="gla_scan",
    )(*args)


def _gla_out_kernel(of_ref, ob_ref, r_ref, g_ref, o_ref, *, hv):
    g = g_ref[...]
    for hd in range(GLA_HEADS):
        sl = slice(hd * hv, (hd + 1) * hv)
        o = of_ref[:, sl].astype(F32) + ob_ref[:, sl].astype(F32)
        o = o * lax.rsqrt(jnp.mean(o * o, axis=-1, keepdims=True) + EPS) * g
        r = r_ref[:, sl]
        o_ref[:, sl] = (o * (r * jax.nn.sigmoid(r))).astype(o_ref.dtype)


def _gla_out(o_f, o_b, z, on_g, hv):
    m, d = o_f.shape
    tm = _pick(m, 256)
    r_col = (z.shape[1] - d) // d
    return pl.pallas_call(
        functools.partial(_gla_out_kernel, hv=hv),
        grid=(m // tm,),
        in_specs=[pl.BlockSpec((tm, d), lambda i: (i, 0)),
                  pl.BlockSpec((tm, d), lambda i: (i, 0)),
                  pl.BlockSpec((tm, d), lambda i: (i, r_col)),
                  pl.BlockSpec((1, hv), lambda i: (0, 0))],
        out_specs=pl.BlockSpec((tm, d), lambda i: (i, 0)),
        out_shape=jax.ShapeDtypeStruct((m, d), BF16),
        compiler_params=_cparams(("parallel",), 48),
        name="gla_out",
    )(o_f, o_b, z, on_g.reshape(1, hv))


def _cos_sin(j, k, n):
    ang = ((j[:, None] * k[None, :]) % n).astype(F32) * (2.0 * math.pi / n)
    return jnp.cos(ang), jnp.sin(ang)


def _dft_tables(n):
    idx = jnp.arange(n, dtype=jnp.int32)
    blk = 64
    if n % blk or n <= 4 * blk:
        return jnp.stack(_cos_sin(idx, idx, n)).astype(BF16)
    ca, sa = _cos_sin(idx, jnp.arange(n // blk, dtype=jnp.int32) * blk, n)
    cb, sb = _cos_sin(idx, jnp.arange(blk, dtype=jnp.int32), n)
    ca, sa, cb, sb = ca[:, :, None], sa[:, :, None], cb[:, None, :], sb[:, None, :]
    cos = (ca * cb - sa * sb).astype(BF16).reshape(n, n)
    sin = (sa * cb + ca * sb).astype(BF16).reshape(n, n)
    return jnp.stack([cos, sin])


def _chan_table(cg):
    half = cg // 2
    cs = _dft_tables(cg)[:, :, :half]
    alt = jnp.where(jnp.arange(cg) % 2 == 0, 1.0, -1.0).astype(BF16)
    nyq = jnp.zeros((cg, NYQ_PAD), BF16).at[:, 0].set(alt)
    return jnp.concatenate([cs[0], cs[1], nyq], axis=1)


def _mirror_weight(w, cg):
    nl, d, n = w.shape
    half = cg // 2
    base = (np.arange(d // cg) * cg)[:, None]
    f = np.arange(half)[None, :]
    mirror = np.where(f == 0, half, (cg - f) % cg)
    rows = np.concatenate([(base + f).reshape(-1), (base + mirror).reshape(-1)])
    return jnp.take(w.astype(BF16), rows, axis=1).reshape(2 * nl, d // 2, n)


def _chan_dft_kernel(a_ref, w_ref, o_ref, n_ref, *, half):
    r = jnp.dot(a_ref[...], w_ref[...], preferred_element_type=F32)
    o_ref[0] = r[:, :half].astype(o_ref.dtype)
    o_ref[1] = r[:, half:2 * half].astype(o_ref.dtype)
    n_ref[...] = r[:, 2 * half:].astype(n_ref.dtype)


def _chan_dft(hb, table):
    m, d = hb.shape
    cg = table.shape[0]
    half, groups = cg // 2, d // cg
    tm = _pick(m, 1024)
    return pl.pallas_call(
        functools.partial(_chan_dft_kernel, half=half),
        grid=(m // tm, groups),
        in_specs=[pl.BlockSpec((tm, cg), lambda i, g: (i, g)),
                  pl.BlockSpec(table.shape, lambda i, g: (0, 0))],
        out_specs=[pl.BlockSpec((2, tm, half), lambda i, g: (0, i, g)),
                   pl.BlockSpec((tm, NYQ_PAD), lambda i, g: (i, g))],
        out_shape=[jax.ShapeDtypeStruct((2, m, groups * half), BF16),
                   jax.ShapeDtypeStruct((m, groups * NYQ_PAD), BF16)],
        compiler_params=_cparams(("parallel", "arbitrary"), 40),
        name="chan_dft",
    )(hb, table)


def _seq_dft_whole_kernel(w_ref, p0_ref, p1_ref, xn_ref, o_ref, *, half, scale):
    cp = jnp.dot(w_ref[0], p0_ref[...], preferred_element_type=F32)
    sp = jnp.dot(w_ref[1], p1_ref[...], preferred_element_type=F32)
    cn = jnp.dot(w_ref[0], xn_ref[...], preferred_element_type=F32)
    o_ref[0] = ((cp - sp) * scale).astype(o_ref.dtype)
    o_ref[1] = ((cp + sp) * scale).astype(o_ref.dtype)
    for g in range(o_ref.shape[2] // half):
        o_ref[1, :, g * half:g * half + 1] = (
            cn[:, g * NYQ_PAD:g * NYQ_PAD + 1] * scale).astype(o_ref.dtype)


def _seq_dft_whole(planes, nyq, wt, *, row0, nseq, t, scale, half):
    _, m, n = planes.shape
    assert row0 % t == 0, "sequences must start on a multiple of their length"
    blk0 = row0 // t
    return pl.pallas_call(
        functools.partial(_seq_dft_whole_kernel, half=half, scale=scale),
        grid=(nseq,),
        in_specs=[pl.BlockSpec(wt.shape, lambda b: (0, 0, 0)),
                  pl.BlockSpec((None, t, n), lambda b: (0, blk0 + b, 0)),
                  pl.BlockSpec((None, t, n), lambda b: (1, blk0 + b, 0)),
                  pl.BlockSpec((t, nyq.shape[1]), lambda b: (blk0 + b, 0))],
        out_specs=pl.BlockSpec((2, t, n), lambda b: (0, blk0 + b, 0)),
        out_shape=jax.ShapeDtypeStruct((2, m, n), BF16),
        compiler_params=_cparams(("parallel",), 40),
        name="seq_dft_whole",
    )(wt, planes, planes, nyq)


def _seq_dft_tiled_kernel(w_ref, p_ref, xn_ref, o_ref, cp_ref, cn_ref, *, scale):
    k, g = pl.program_id(2), pl.program_id(3)

    @pl.when(k == 0)
    def _():
        cp_ref[g] = jnp.dot(w_ref[...], p_ref[...], preferred_element_type=F32)
        cn_ref[g] = jnp.dot(w_ref[...], xn_ref[...], preferred_element_type=F32)

    @pl.when(k == 1)
    def _():
        sp = jnp.dot(w_ref[...], p_ref[...], preferred_element_type=F32)
        cp = cp_ref[g]
        o_ref[0] = ((cp - sp) * scale).astype(o_ref.dtype)
        o_ref[1] = ((cp + sp) * scale).astype(o_ref.dtype)
        o_ref[1, :, 0:1] = (cn_ref[g][:, 0:1] * scale).astype(o_ref.dtype)


def _seq_dft_tiled(planes, nyq, wt, y_prev, *, row0, nseq, t, scale, half):
    _, m, n = planes.shape
    groups = n // half
    tm = _pick(t, 1024)
    assert row0 % t == 0, "sequences must start on a multiple of their length"
    blk_t = lambda b: (row0 + b * t) // t
    blk_m = lambda b, i: (row0 + b * t) // tm + i
    return pl.pallas_call(
        lambda w, p, xn, y_any, o, cp, cn: _seq_dft_tiled_kernel(w, p, xn, o, cp, cn, scale=scale),
        grid=(nseq, t // tm, 2, groups),
        in_specs=[pl.BlockSpec((None, tm, t), lambda b, i, k, g: (k, i, 0)),
                  pl.BlockSpec((None, t, half), lambda b, i, k, g: (k, blk_t(b), g)),
                  pl.BlockSpec((t, NYQ_PAD), lambda b, i, k, g: (blk_t(b), g)),
                  pl.BlockSpec(memory_space=pl.ANY)],
        out_specs=pl.BlockSpec((2, tm, half), lambda b, i, k, g: (0, blk_m(b, i), g * k)),
        out_shape=jax.ShapeDtypeStruct((2, m, n), BF16),
        scratch_shapes=[pltpu.VMEM((groups, tm, half), F32),
                        pltpu.VMEM((groups, tm, NYQ_PAD), F32)],
        input_output_aliases={3: 0},
        compiler_params=_cparams(("parallel", "parallel", "arbitrary", "arbitrary"), 56),
        name="seq_dft_tiled",
    )(wt, planes, nyq, y_prev)


def _pos_embed_2d(n_tok, d):
    rows = n_tok // GRID_W
    quarter = d // 4
    omega = 1.0 / (10000.0 ** (jnp.arange(quarter, dtype=F32) / quarter))
    er = jnp.arange(rows, dtype=F32)[:, None] * omega
    ec = jnp.arange(GRID_W, dtype=F32)[:, None] * omega
    er = jnp.concatenate([jnp.sin(er), jnp.cos(er)], axis=-1)
    ec = jnp.concatenate([jnp.sin(ec), jnp.cos(ec)], axis=-1)
    half = d // 2
    emb = jnp.concatenate([jnp.broadcast_to(er[:, None, :], (rows, GRID_W, half)),
                           jnp.broadcast_to(ec[None, :, :], (rows, GRID_W, half))], axis=-1)
    return emb.reshape(rows * GRID_W, d)


def kernel(x_prompt, x_sample, state_fwd, state_bwd, c, c_ctx, norm1_g, norm2_g, w_mod, b_mod,
           gla_w_in, gla_wa1_f, gla_wa2_f, gla_ba_f, gla_wa1_b, gla_wa2_b, gla_ba_b,
           gla_onorm_g, gla_w_out, fnet_w, mlp_w1, mlp_w2, final_g):
    batch, seq, d = x_prompt.shape
    dec_batch, dec_seq, _ = x_sample.shape
    depth = w_mod.shape[0]
    n_p, n_l = batch * seq, dec_batch * dec_seq
    h = GLA_HEADS
    hk, hv = gla_wa2_f.shape[2] // h, gla_onorm_g.shape[1]
    rank = gla_wa1_f.shape[2]
    cg = d // F_GROUPS

    n_cond = -(-(1 + dec_batch) // 8) * 8
    cond = jnp.zeros((n_cond, d), F32).at[0].set(c_ctx).at[1:1 + dec_batch].set(c)
    mods = _modulation(cond, w_mod, b_mod)
    mods3 = mods.reshape(depth * n_cond * N_MOD, 1, d)
    rows = _Rows(n_p, dec_seq, n_cond)

    x, hb0 = _embed(x_prompt.reshape(n_p, d), x_sample.reshape(n_l, d), _pos_embed_2d(dec_seq, d),
                    norm1_g[0], mods3, rows)

    tables = _gla_tables(GLA_CHUNK)
    dft_c = _chan_table(cg)
    dft_tp = _dft_tables(seq)
    dft_tl = _dft_tables(dec_seq)
    w_in, w_out = gla_w_in.astype(BF16), gla_w_out.astype(BF16)
    w_fnet = _mirror_weight(fnet_w, cg)
    w2 = mlp_w2.astype(BF16)
    n_gla = gla_w_in.shape[0]
    wa1 = jnp.zeros((n_gla, d, A1_WIDTH), F32)
    wa1 = wa1.at[:, :, :rank].set(gla_wa1_f).at[:, :, rank:2 * rank].set(gla_wa1_b).astype(BF16)
    new_states = None
    for i in range(depth):
        j = i // 2
        hb = hb0 if i == 0 else _norm_mod(x, norm1_g[i], mods3, rows, i, 0)
        resid = lambda which: (x, mods3, rows, i, which)
        if i % 2 == 0:
            z = _matmul(hb, w_in, w_layer=j, out_dtype=F32)
            wa2 = jnp.zeros((2 * rank, 2 * h * hk), F32)
            wa2 = wa2.at[:rank, :h * hk].set(gla_wa2_f[j])
            wa2 = wa2.at[rank:2 * rank, h * hk:].set(gla_wa2_b[j])
            ba = jnp.concatenate([gla_ba_f[j], gla_ba_b[j]]).reshape(1, -1)
            g1, g2 = _log_gates(hb, wa1, j, wa2, ba)
            common = dict(hk=hk, hv=hv, layer=j, n_layers=n_gla)
            o_f, o_b, s_f, s_b = _gla_scan(z, g1, g2, tables, row0=0, nseq=batch, t=seq,
                                           emit_final=True, final=new_states, **common)
            new_states = (s_f, s_b)
            o_f, o_b = _gla_scan(z, g1, g2, tables, row0=n_p, nseq=dec_batch,
                                 t=dec_seq, prev=(o_f, o_b), init=(state_fwd, state_bwd),
                                 **common)
            y = _gla_out(o_f, o_b, z, gla_onorm_g[j], hv)
            x = _matmul(y, w_out, w_layer=j, out_dtype=F32, epilogue="resid", resid=resid(2))
        else:
            planes, nyq = _chan_dft(hb, dft_c)
            y = _seq_dft_whole(planes, nyq, dft_tp, row0=0, nseq=batch, t=seq, half=cg // 2,
                               scale=1.0 / math.sqrt(seq * cg))
            y = _seq_dft_tiled(planes, nyq, dft_tl, y, row0=n_p, nseq=dec_batch, t=dec_seq,
                               half=cg // 2, scale=1.0 / math.sqrt(dec_seq * cg))
            x = _matmul(y, w_fnet, w_layer=j, out_dtype=F32, epilogue="resid", resid=resid(2))
        hb = _norm_mod(x, norm2_g[i], mods3, rows, i, 3)
        a = _matmul(hb, mlp_w1, w_layer=i, out_dtype=BF16, epilogue="relu2", tm=2048, tn=512)
        x = _matmul(a, w2, w_layer=i, out_dtype=F32, epilogue="resid", resid=resid(5))
    y_prompt = _final_norm(x, final_g, 0, n_p).reshape(batch, seq, d)
    y_sample = _final_norm(x, final_g, n_p, n_l).reshape(dec_batch, dec_seq, d)
    return (y_prompt, y_sample) + tuple(new_states)
```

```python
import functools
import math

import numpy as np
import jax
import jax.numpy as jnp
from jax import lax
from jax.experimental import pallas as pl
from jax.experimental.pallas import tpu as pltpu

F32 = jnp.float32
BF16 = jnp.bfloat16

EPS = 1e-6
GLA_HEADS = 8
GATE_NORM = 16.0
LOG2_E = 1.4426950408889634
F_GROUPS = 4
GRID_W = 64
N_MOD = 6
GLA_CHUNK = 128
GLA_HEADS_PER_STEP = 4
GLA_TABLE_LEVELS = 3
A1_WIDTH = 128
NYQ_PAD = 128


def _pick(dim, pref):
    t = min(dim, pref)
    while dim % t:
        t //= 2
    return t


def _cparams(semantics, vmem_mib):
    return pltpu.CompilerParams(dimension_semantics=semantics,
                                vmem_limit_bytes=int(vmem_mib) << 20)


class _Rows:
    def __init__(self, n_p, t_l, n_cond):
        self.n_p, self.t_l, self.n_cond = n_p, t_l, n_cond

    def cond(self, row0):
        return jnp.where(row0 < self.n_p, 0, 1 + (row0 - self.n_p) // self.t_l)

    def mod_row(self, layer, which, row0):
        return (layer * self.n_cond + self.cond(row0)) * N_MOD + which


def _mod_kernel(c_ref, w_ref, b_ref, o_ref):
    c = c_ref[...]
    a = (c * jax.nn.sigmoid(c)).astype(BF16)
    o_ref[...] = jnp.dot(a, w_ref[...].astype(BF16), preferred_element_type=F32) + b_ref[...]


def _modulation(cond, w_mod, b_mod):
    depth, d, n = w_mod.shape
    r = cond.shape[0]
    tn = _pick(n, 512)
    return pl.pallas_call(
        _mod_kernel,
        grid=(depth, n // tn),
        in_specs=[pl.BlockSpec((r, d), lambda l, j: (0, 0)),
                  pl.BlockSpec((None, d, tn), lambda l, j: (l, 0, j)),
                  pl.BlockSpec((None, 1, tn), lambda l, j: (l, 0, j))],
        out_specs=pl.BlockSpec((None, r, tn), lambda l, j: (l, 0, j)),
        out_shape=jax.ShapeDtypeStruct((depth, r, n), F32),
        compiler_params=_cparams(("parallel", "parallel"), 40),
        name="modulation",
    )(cond, w_mod, b_mod.reshape(depth, 1, n))


def _norm_mod_rows(x, g_ref, scale_ref, shift_ref):
    y = x * lax.rsqrt(jnp.mean(x * x, axis=-1, keepdims=True) + EPS)
    return (y * g_ref[...]) * (1.0 + scale_ref[...]) + shift_ref[...]


def _embed_kernel(xp_ref, xl_ref, pos_ref, g_ref, scale_ref, shift_ref, x_ref, h_ref, *, n_pblk):
    i = pl.program_id(0)

    @pl.when(i < n_pblk)
    def _():
        x_ref[...] = xp_ref[...]

    @pl.when(i >= n_pblk)
    def _():
        x_ref[...] = xl_ref[...] + pos_ref[...]

    h_ref[...] = _norm_mod_rows(x_ref[...], g_ref, scale_ref, shift_ref).astype(h_ref.dtype)


def _embed(xp2, xl2, pos, g, mods3, rows):
    n_p, d = xp2.shape
    n_l = xl2.shape[0]
    t_l = pos.shape[0]
    tm = _pick(math.gcd(n_p, t_l), 256)
    n_pblk, n_posblk, n_seq = n_p // tm, t_l // tm, n_l // t_l
    lat = lambda i: jnp.maximum(i - n_pblk, 0)
    pos_blk = lambda i: lat(i) // n_seq
    lat_blk = lambda i: (lat(i) % n_seq) * n_posblk + pos_blk(i)
    row_blk = lambda i: jnp.where(i < n_pblk, i, n_pblk + lat_blk(i))
    mod_spec = lambda which: pl.BlockSpec(
        (None, 1, d), lambda i: (rows.mod_row(0, which, row_blk(i) * tm), 0, 0))
    return pl.pallas_call(
        functools.partial(_embed_kernel, n_pblk=n_pblk),
        grid=((n_p + n_l) // tm,),
        in_specs=[pl.BlockSpec((tm, d), lambda i: (jnp.minimum(i, n_pblk - 1), 0)),
                  pl.BlockSpec((tm, d), lambda i: (lat_blk(i), 0)),
                  pl.BlockSpec((tm, d), lambda i: (pos_blk(i), 0)),
                  pl.BlockSpec((1, d), lambda i: (0, 0)),
                  mod_spec(1), mod_spec(0)],
        out_specs=[pl.BlockSpec((tm, d), lambda i: (row_blk(i), 0))] * 2,
        out_shape=[jax.ShapeDtypeStruct((n_p + n_l, d), F32),
                   jax.ShapeDtypeStruct((n_p + n_l, d), BF16)],
        compiler_params=_cparams(("parallel",), 48),
        name="embed",
    )(xp2, xl2, pos, g.reshape(1, d), mods3, mods3)


def _norm_mod_kernel(x_ref, g_ref, scale_ref, shift_ref, o_ref):
    o_ref[...] = _norm_mod_rows(x_ref[...], g_ref, scale_ref, shift_ref).astype(o_ref.dtype)


def _norm_mod(x, g, mods3, rows, layer, which_shift):
    m, d = x.shape
    tm = _pick(math.gcd(rows.n_p, rows.t_l), 512)
    mod_spec = lambda which: pl.BlockSpec(
        (None, 1, d), lambda i: (rows.mod_row(layer, which, i * tm), 0, 0))
    return pl.pallas_call(
        _norm_mod_kernel,
        grid=(m // tm,),
        in_specs=[pl.BlockSpec((tm, d), lambda i: (i, 0)),
                  pl.BlockSpec((1, d), lambda i: (0, 0)),
                  mod_spec(which_shift + 1), mod_spec(which_shift)],
        out_specs=pl.BlockSpec((tm, d), lambda i: (i, 0)),
        out_shape=jax.ShapeDtypeStruct((m, d), BF16),
        compiler_params=_cparams(("parallel",), 48),
        name="norm_mod",
    )(x, g.reshape(1, d), mods3, mods3)


def _final_norm_kernel(x_ref, g_ref, o_ref):
    x = x_ref[...]
    y = x * lax.rsqrt(jnp.mean(x * x, axis=-1, keepdims=True) + EPS)
    o_ref[...] = y * g_ref[...]


def _final_norm(x, g, row0, nrows):
    d = x.shape[1]
    tm = _pick(math.gcd(row0, nrows) if row0 else nrows, 512)
    blk0 = row0 // tm
    return pl.pallas_call(
        _final_norm_kernel,
        grid=(nrows // tm,),
        in_specs=[pl.BlockSpec((tm, d), lambda i: (blk0 + i, 0)),
                  pl.BlockSpec((1, d), lambda i: (0, 0))],
        out_specs=pl.BlockSpec((tm, d), lambda i: (i, 0)),
        out_shape=jax.ShapeDtypeStruct((nrows, d), F32),
        compiler_params=_cparams(("parallel",), 48),
        name="final_norm",
    )(x, g.reshape(1, d))


def _mm_kernel(*refs, nk, epilogue):
    if epilogue == "resid":
        a_ref, b_ref, x_ref, gate_ref, o_ref = refs[:5]
        scratch = refs[5:]
    else:
        a_ref, b_ref, o_ref = refs[:3]
        scratch = refs[3:]

    def finish(acc):
        if epilogue == "relu2":
            r = jnp.maximum(acc, 0.0)
            o_ref[...] = (r * r).astype(o_ref.dtype)
        elif epilogue == "resid":
            o_ref[...] = x_ref[...] + gate_ref[...] * acc
        else:
            o_ref[...] = acc.astype(o_ref.dtype)

    def product():
        b = b_ref[...]
        if b.dtype != BF16:
            b = b.astype(BF16)
        return jnp.dot(a_ref[...], b, preferred_element_type=F32)

    if nk == 1:
        finish(product())
        return
    acc_ref = o_ref if epilogue == "resid" else scratch[0]
    k = pl.program_id(2)

    @pl.when(k == 0)
    def _():
        acc_ref[...] = product()

    @pl.when((k > 0) & (k < nk - 1))
    def _():
        acc_ref[...] = acc_ref[...] + product()

    @pl.when(k == nk - 1)
    def _():
        finish(acc_ref[...] + product())


def _matmul(a, b, *, out_dtype, epilogue="plain", resid=None, w_layer=0,
            tm=1024, tn=1024, tk=4096):
    planes = a.shape[0] if a.ndim == 3 else 1
    m, kdim = a.shape[-2:]
    n = b.shape[2]
    tm, tn, tk = _pick(m, tm), _pick(n, tn), _pick(kdim, tk)
    if resid is not None:
        tm = _pick(math.gcd(resid[2].n_p, resid[2].t_l), tm)
    kp = kdim // tk
    nk = planes * kp
    if a.ndim == 3:
        in_specs = [pl.BlockSpec((None, tm, tk), lambda i, j, k: (k // kp, i, k % kp)),
                    pl.BlockSpec((None, tk, tn),
                                 lambda i, j, k: (w_layer * planes + k // kp, k % kp, j))]
    else:
        a_mode = dict(pipeline_mode=pl.Buffered(1)) if (nk == 1 and b.dtype != BF16) else {}
        in_specs = [pl.BlockSpec((tm, tk), lambda i, j, k: (i, k), **a_mode),
                    pl.BlockSpec((None, tk, tn), lambda i, j, k: (w_layer, k, j))]
    args = [a, b]
    if epilogue == "resid":
        x, mods3, rows, layer, which = resid
        in_specs += [pl.BlockSpec((tm, tn), lambda i, j, k: (i, j)),
                     pl.BlockSpec((None, 1, tn),
                                  lambda i, j, k: (rows.mod_row(layer, which, i * tm), 0, j))]
        args += [x, mods3]
    out_bytes = jnp.dtype(out_dtype).itemsize
    b_bytes = jnp.dtype(b.dtype).itemsize
    vmem = 2 * (tm * tk * 2 + tk * tn * b_bytes) + 2 * tm * tn * out_bytes
    if b.dtype != BF16:
        vmem += tk * tn * 2 - (tm * tk * 2 if nk == 1 else 0)
    if epilogue == "resid":
        vmem += 2 * tm * tn * 4
    use_acc = nk > 1 and epilogue != "resid"
    if use_acc:
        vmem += tm * tn * 4
    vmem += 2 * tm * tn * 4
    return pl.pallas_call(
        functools.partial(_mm_kernel, nk=nk, epilogue=epilogue),
        grid=(m // tm, n // tn, nk),
        in_specs=in_specs,
        out_specs=pl.BlockSpec((tm, tn), lambda i, j, k: (i, j)),
        out_shape=jax.ShapeDtypeStruct((m, n), out_dtype),
        scratch_shapes=[pltpu.VMEM((tm, tn), F32)] if use_acc else [],
        compiler_params=_cparams(("parallel", "parallel", "arbitrary"),
                                 min(60, vmem / 2 ** 20 + 6)),
        name="matmul_" + epilogue,
    )(*args)


def _gla_tables(c):
    t = np.arange(c)[:, None]
    r = np.arange(c)[None, :]
    a_f = [(r <= t)]
    a_b = [(r >= t)]
    for lev in range(1, GLA_TABLE_LEVELS + 1):
        p, hh = 1 << lev, 1 << (lev - 1)
        mid = (t // p) * p + hh
        left = (t % p) < hh
        a_f.append(np.where(left, (r > t) & (r <= mid - 1), (r >= mid) & (r <= t)))
        a_b.append(np.where(left, (r >= t) & (r < mid), (r >= mid) & (r < t)))
    a_all = np.stack([np.concatenate(a_f, 0), np.concatenate(a_b, 0)]).astype(np.float32)
    a_all = np.tile(a_all, (1, 1, 2))
    x = np.bitwise_xor(t, r)
    lvl = np.where(x == 0, 0, np.floor(np.log2(np.maximum(x, 1))).astype(np.int64) + 1)
    lv = np.stack([np.where(r <= t, lvl, -1), np.where(r >= t, lvl, -1)]).astype(np.int32)
    return jnp.asarray(a_all, BF16), jnp.asarray(lv)


def _split2(x):
    hi = x.astype(BF16)
    return hi, (x - hi.astype(F32)).astype(BF16)


def _gate_kernel(h_ref, wa1_ref, w2_ref, ba_ref, g1_ref, g2_ref, *, slot):
    a1 = jnp.dot(h_ref[...], wa1_ref[...], preferred_element_type=F32)
    hi = a1.astype(BF16).astype(F32)
    a_cat = (hi + pltpu.roll(a1 - hi, slot, 1) + pltpu.roll(hi, 2 * slot, 1)).astype(BF16)
    n = g1_ref.shape[1]
    tn = _pick(n, 512)
    for j in range(n // tn):
        cols = slice(j * tn, (j + 1) * tn)
        y = jnp.dot(a_cat, w2_ref[:, cols], preferred_element_type=F32) + ba_ref[:, cols]
        g = ((jnp.minimum(y, 0.0) - jnp.log2(1.0 + jnp.exp2(-jnp.abs(y))))
             * (1.0 / GATE_NORM))
        g1_ref[:, cols], g2_ref[:, cols] = _split2(g)


def _log_gates(hb, wa1, w_layer, wa2, ba):
    m, d = hb.shape
    slot, n = wa2.shape
    r = wa1.shape[2]
    assert 3 * slot <= r
    tm = _pick(m, 256)
    w2hi, w2lo = _split2(wa2 * LOG2_E)
    w2 = jnp.zeros((r, n), BF16).at[:3 * slot].set(jnp.concatenate([w2hi, w2hi, w2lo]))
    ba = ba * LOG2_E
    full = lambda shape: pl.BlockSpec(shape, lambda i: (0,) * len(shape))
    return pl.pallas_call(
        functools.partial(_gate_kernel, slot=slot),
        grid=(m // tm,),
        in_specs=[pl.BlockSpec((tm, d), lambda i: (i, 0)),
                  pl.BlockSpec((None, d, r), lambda i: (w_layer, 0, 0)),
                  full((r, n)), full((1, n))],
        out_specs=[pl.BlockSpec((tm, n), lambda i: (i, 0))] * 2,
        out_shape=[jax.ShapeDtypeStruct((m, n), BF16)] * 2,
        compiler_params=_cparams(("parallel",), 48),
        name="log_gates",
    )(hb, wa1, w2, ba)


def _dot_nt(a, b):
    return lax.dot_general(a, b, (((1,), (1,)), ((), ())), preferred_element_type=F32)


def _gla_chunks(chains, a_all_ref, lv_ref, *, c, hk):
    nlev = c.bit_length() - 1
    n = range(len(chains))
    dirs = [ch[0] for ch in chains]
    edge_row = {0: c - 1, 1: 0}
    ee = [jnp.dot(a_all_ref[dirs[i]], jnp.concatenate(chains[i][4](), axis=0),
                  preferred_element_type=F32) for i in n]
    q = lambda i: chains[i][1]()
    k = lambda i: chains[i][2]()
    scale = hk ** -0.5
    qb = [q(i).astype(BF16) for i in n]
    kb = [k(i).astype(BF16) for i in n]
    att = [jnp.where(lv_ref[dirs[i]] == 0, _dot_nt(qb[i], kb[i]), 0.0) for i in n]
    def level_exponent(i, lev):
        if lev <= GLA_TABLE_LEVELS:
            return ee[i][lev * c:(lev + 1) * c]
        b = ee[i][:c]
        p, ref0 = 1 << lev, (1 << (lev - 1)) - 1 + dirs[i]
        ref = [jnp.broadcast_to(b[p0 + ref0:p0 + ref0 + 1], (p, hk)) for p0 in range(0, c, p)]
        return -jnp.abs(b - (ref[0] if len(ref) == 1 else jnp.concatenate(ref, axis=0)))

    for lev in range(1, nlev + 1):
        for i in n:
            u = jnp.exp2(level_exponent(i, lev)).astype(BF16)
            att[i] = att[i] + jnp.where(lv_ref[dirs[i]] == lev,
                                        _dot_nt(qb[i] * u, kb[i] * u), 0.0)
    for i in n:
        b = ee[i][:c]
        vb = chains[i][3]().astype(BF16)
        st = chains[i][5][...]
        o = jnp.dot((q(i) * jnp.exp2(b)).astype(BF16), st.astype(BF16), preferred_element_type=F32)
        o = o + jnp.dot(att[i].astype(BF16), vb, preferred_element_type=F32)
        chains[i][6](o * scale)
        er = edge_row[dirs[i]]
        edge = b[er:er + 1]
        kt = jnp.transpose(k(i) * jnp.exp2(edge - b)).astype(BF16)
        dcol = jnp.transpose(jnp.broadcast_to(jnp.exp2(edge), (128, hk)))[:, :1]
        chains[i][5][...] = dcol * st + jnp.dot(kt, vb, preferred_element_type=F32)


def _gla_kernel(*refs, c, hk, hpb, has_init, emit_final):
    (qf, kf, vf, g1f, g2f, qb, kb, vb, g1b, g2b, aall, lv) = refs[:12]
    pos = 12
    if has_init:
        s0f, s0b = refs[pos:pos + 2]
        pos += 2
    of, ob = refs[pos:pos + 2]
    pos += 2
    if emit_final:
        sff, sfb = refs[pos:pos + 2]
        pos += 2
    sf, sb = refs[pos:pos + 2]
    step = pl.program_id(2)

    @pl.when(step == 0)
    def _():
        if has_init:
            sf[...] = s0f[...]
            sb[...] = s0b[...]
        else:
            sf[...] = jnp.zeros_like(sf)
            sb[...] = jnp.zeros_like(sb)

    hv = of.shape[1] // hpb

    chains = []
    for d, (q, k, v, g1, g2, s, o) in enumerate(((qf, kf, vf, g1f, g2f, sf, of),
                                                 (qb, kb, vb, g1b, g2b, sb, ob))):
        for hh in range(hpb):
            ksl = slice(hh * hk, (hh + 1) * hk)
            vsl = slice(hh * hv, (hh + 1) * hv)

            def store(val, o=o, vsl=vsl):
                o[:, vsl] = val.astype(o.dtype)

            chains.append((d,
                           lambda q=q, ksl=ksl: q[:, ksl],
                           lambda k=k, ksl=ksl: k[:, ksl],
                           lambda v=v, vsl=vsl: v[:, vsl],
                           lambda g1=g1, g2=g2, ksl=ksl: (g1[:, ksl], g2[:, ksl]),
                           s.at[hh], store))
    _gla_chunks(chains, aall, lv, c=c, hk=hk)

    if emit_final:
        @pl.when(step == pl.num_programs(2) - 1)
        def _():
            sff[...] = sf[...]
            sfb[...] = sb[...]


def _gla_scan(z, g1, g2, tables, *, row0, nseq, t, hk, hv,
              layer, n_layers, init=None, final=None, emit_final=False, prev=None):
    m = z.shape[0]
    h = GLA_HEADS
    hpb = GLA_HEADS_PER_STEP
    c = GLA_CHUNK if t % GLA_CHUNK == 0 else t
    nchunk = t // c
    rb0 = row0 // c
    a_all, lv = tables
    fwd = lambda b_, c_: rb0 + b_ * nchunk + c_
    bwd = lambda b_, c_: rb0 + b_ * nchunk + (nchunk - 1 - c_)
    k_col0 = h // hpb
    v_col0 = (2 * h * hk) // (hpb * hv)

    def stream(rowfn, d):
        gate = pl.BlockSpec((c, hpb * hk), lambda b_, h_, c_: (rowfn(b_, c_), d * k_col0 + h_))
        return [pl.BlockSpec((c, hpb * hk), lambda b_, h_, c_: (rowfn(b_, c_), h_)),
                pl.BlockSpec((c, hpb * hk), lambda b_, h_, c_: (rowfn(b_, c_), k_col0 + h_)),
                pl.BlockSpec((c, hpb * hv), lambda b_, h_, c_: (rowfn(b_, c_), v_col0 + h_)),
                gate, gate]

    in_specs = (stream(fwd, 0) + stream(bwd, 1)
                + [pl.BlockSpec(a_all.shape, lambda b_, h_, c_: (0, 0, 0)),
                   pl.BlockSpec(lv.shape, lambda b_, h_, c_: (0, 0, 0))])
    args = [z, z, z, g1, g2, z, z, z, g1, g2, a_all, lv]
    state_spec = pl.BlockSpec((None, None, hpb, hk, hv),
                              lambda b_, h_, c_: (b_, layer, h_, 0, 0))
    if init is not None:
        in_specs += [state_spec, state_spec]
        args += list(init)
    out_specs = [pl.BlockSpec((c, hpb * hv), lambda b_, h_, c_: (fwd(b_, c_), h_)),
                 pl.BlockSpec((c, hpb * hv), lambda b_, h_, c_: (bwd(b_, c_), h_))]
    out_shape = [jax.ShapeDtypeStruct((m, h * hv), BF16)] * 2
    if emit_final:
        out_specs += [state_spec, state_spec]
        out_shape += [jax.ShapeDtypeStruct((nseq, n_layers, h, hk, hv), F32)] * 2
    aliases = {}
    n_blocked = len(args)
    for bufs, out0 in ((prev, 0), (final, 2)):
        if bufs is not None:
            aliases.update({len(args): out0, len(args) + 1: out0 + 1})
            in_specs += [pl.BlockSpec(memory_space=pl.ANY)] * 2
            args += list(bufs)

    def body(*refs):
        refs = refs[:n_blocked] + refs[len(args):]
        _gla_kernel(*refs, c=c, hk=hk, hpb=hpb, has_init=init is not None,
                    emit_final=emit_final)

    return pl.pallas_call(
        body,
        grid=(nseq, h // hpb, nchunk),
        in_specs=in_specs,
        out_specs=out_specs,
        out_shape=out_shape,
        scratch_shapes=[pltpu.VMEM((hpb, hk, hv), F32)] * 2,
        input_output_aliases=aliases,
        compiler_params=_cparams(("parallel", "parallel", "arbitrary"), 56),
        name="gla_scan",
    )(*args)


def _gla_out_kernel(of_ref, ob_ref, r_ref, g_ref, o_ref, *, hv):
    g = g_ref[...]
    for hd in range(GLA_HEADS):
        sl = slice(hd * hv, (hd + 1) * hv)
        o = of_ref[:, sl].astype(F32) + ob_ref[:, sl].astype(F32)
        o = o * lax.rsqrt(jnp.mean(o * o, axis=-1, keepdims=True) + EPS) * g
        r = r_ref[:, sl]
        o_ref[:, sl] = (o * (r * jax.nn.sigmoid(r))).astype(o_ref.dtype)


def _gla_out(o_f, o_b, z, on_g, hv):
    m, d = o_f.shape
    tm = _pick(m, 256)
    r_col = (z.shape[1] - d) // d
    return pl.pallas_call(
        functools.partial(_gla_out_kernel, hv=hv),
        grid=(m // tm,),
        in_specs=[pl.BlockSpec((tm, d), lambda i: (i, 0)),
                  pl.BlockSpec((tm, d), lambda i: (i, 0)),
                  pl.BlockSpec((tm, d), lambda i: (i, r_col)),
                  pl.BlockSpec((1, hv), lambda i: (0, 0))],
        out_specs=pl.BlockSpec((tm, d), lambda i: (i, 0)),
        out_shape=jax.ShapeDtypeStruct((m, d), BF16),
        compiler_params=_cparams(("parallel",), 48),
        name="gla_out",
    )(o_f, o_b, z, on_g.reshape(1, hv))


def _cos_sin(j, k, n):
    ang = ((j[:, None] * k[None, :]) % n).astype(F32) * (2.0 * math.pi / n)
    return jnp.cos(ang), jnp.sin(ang)


def _dft_tables(n):
    idx = jnp.arange(n, dtype=jnp.int32)
    blk = 64
    if n % blk or n <= 4 * blk:
        return jnp.stack(_cos_sin(idx, idx, n)).astype(BF16)
    ca, sa = _cos_sin(idx, jnp.arange(n // blk, dtype=jnp.int32) * blk, n)
    cb, sb = _cos_sin(idx, jnp.arange(blk, dtype=jnp.int32), n)
    ca, sa, cb, sb = ca[:, :, None], sa[:, :, None], cb[:, None, :], sb[:, None, :]
    cos = (ca * cb - sa * sb).astype(BF16).reshape(n, n)
    sin = (sa * cb + ca * sb).astype(BF16).reshape(n, n)
    return jnp.stack([cos, sin])


def _chan_table(cg):
    half = cg // 2
    cs = _dft_tables(cg)[:, :, :half]
    alt = jnp.where(jnp.arange(cg) % 2 == 0, 1.0, -1.0).astype(BF16)
    nyq = jnp.zeros((cg, NYQ_PAD), BF16).at[:, 0].set(alt)
    return jnp.concatenate([cs[0], cs[1], nyq], axis=1)


def _mirror_weight_kernel(w_ref, sel_ref, o_ref):
    w = w_ref[...].astype(BF16)
    o_ref[0] = w[:o_ref.shape[1]]
    o_ref[1] = jnp.dot(sel_ref[...], w, preferred_element_type=F32).astype(o_ref.dtype)


def _mirror_weight(w, cg):
    nl, d, n = w.shape
    half, groups = cg // 2, d // cg
    f = np.arange(half)
    sel = np.zeros((half, cg), np.float32)
    sel[f, np.where(f == 0, half, (cg - f) % cg)] = 1.0
    tn = _pick(n, 1024)
    out = pl.pallas_call(
        _mirror_weight_kernel,
        grid=(nl, groups, n // tn),
        in_specs=[pl.BlockSpec((None, cg, tn), lambda l, g, j: (l, g, j)),
                  pl.BlockSpec((half, cg), lambda l, g, j: (0, 0))],
        out_specs=pl.BlockSpec((None, 2, half, tn), lambda l, g, j: (l, 0, g, j)),
        out_shape=jax.ShapeDtypeStruct((nl, 2, d // 2, n), BF16),
        compiler_params=_cparams(("parallel", "parallel", "parallel"), 32),
        name="mirror_weight",
    )(w, jnp.asarray(sel, BF16))
    return out.reshape(2 * nl, d // 2, n)


def _chan_dft_kernel(a_ref, w_ref, o_ref, n_ref, *, half):
    r = jnp.dot(a_ref[...], w_ref[...], preferred_element_type=F32)
    o_ref[0] = r[:, :half].astype(o_ref.dtype)
    o_ref[1] = r[:, half:2 * half].astype(o_ref.dtype)
    n_ref[...] = r[:, 2 * half:].astype(n_ref.dtype)


def _chan_dft(hb, table):
    m, d = hb.shape
    cg = table.shape[0]
    half, groups = cg // 2, d // cg
    tm = _pick(m, 1024)
    return pl.pallas_call(
        functools.partial(_chan_dft_kernel, half=half),
        grid=(m // tm, groups),
        in_specs=[pl.BlockSpec((tm, cg), lambda i, g: (i, g)),
                  pl.BlockSpec(table.shape, lambda i, g: (0, 0))],
        out_specs=[pl.BlockSpec((2, tm, half), lambda i, g: (0, i, g)),
                   pl.BlockSpec((tm, NYQ_PAD), lambda i, g: (i, g))],
        out_shape=[jax.ShapeDtypeStruct((2, m, groups * half), BF16),
                   jax.ShapeDtypeStruct((m, groups * NYQ_PAD), BF16)],
        compiler_params=_cparams(("parallel", "arbitrary"), 40),
        name="chan_dft",
    )(hb, table)


def _seq_dft_whole_kernel(w_ref, p0_ref, p1_ref, xn_ref, o_ref, *, half, scale):
    cp = jnp.dot(w_ref[0], p0_ref[...], preferred_element_type=F32)
    sp = jnp.dot(w_ref[1], p1_ref[...], preferred_element_type=F32)
    cn = jnp.dot(w_ref[0], xn_ref[...], preferred_element_type=F32)
    o_ref[0] = ((cp - sp) * scale).astype(o_ref.dtype)
    o_ref[1] = ((cp + sp) * scale).astype(o_ref.dtype)
    for g in range(o_ref.shape[2] // half):
        o_ref[1, :, g * half:g * half + 1] = (
            cn[:, g * NYQ_PAD:g * NYQ_PAD + 1] * scale).astype(o_ref.dtype)


def _seq_dft_whole(planes, nyq, wt, *, row0, nseq, t, scale, half):
    _, m, n = planes.shape
    assert row0 % t == 0, "sequences must start on a multiple of their length"
    blk0 = row0 // t
    return pl.pallas_call(
        functools.partial(_seq_dft_whole_kernel, half=half, scale=scale),
        grid=(nseq,),
        in_specs=[pl.BlockSpec(wt.shape, lambda b: (0, 0, 0)),
                  pl.BlockSpec((None, t, n), lambda b: (0, blk0 + b, 0)),
                  pl.BlockSpec((None, t, n), lambda b: (1, blk0 + b, 0)),
                  pl.BlockSpec((t, nyq.shape[1]), lambda b: (blk0 + b, 0))],
        out_specs=pl.BlockSpec((2, t, n), lambda b: (0, blk0 + b, 0)),
        out_shape=jax.ShapeDtypeStruct((2, m, n), BF16),
        compiler_params=_cparams(("parallel",), 40),
        name="seq_dft_whole",
    )(wt, planes, planes, nyq)


def _seq_dft_tiled_kernel(w_ref, p_ref, xn_ref, o_ref, cp_ref, cn_ref, *, scale):
    k, g = pl.program_id(2), pl.program_id(3)

    @pl.when(k == 0)
    def _():
        cp_ref[g] = jnp.dot(w_ref[...], p_ref[...], preferred_element_type=F32)
        cn_ref[g] = jnp.dot(w_ref[...], xn_ref[...], preferred_element_type=F32)

    @pl.when(k == 1)
    def _():
        sp = jnp.dot(w_ref[...], p_ref[...], preferred_element_type=F32)
        cp = cp_ref[g]
        o_ref[0] = ((cp - sp) * scale).astype(o_ref.dtype)
        o_ref[1] = ((cp + sp) * scale).astype(o_ref.dtype)
        o_ref[1, :, 0:1] = (cn_ref[g][:, 0:1] * scale).astype(o_ref.dtype)


def _seq_dft_tiled(planes, nyq, wt, y_prev, *, row0, nseq, t, scale, half):
    _, m, n = planes.shape
    groups = n // half
    tm = _pick(t, 1024)
    assert row0 % t == 0, "sequences must start on a multiple of their length"
    blk_t = lambda b: (row0 + b * t) // t
    blk_m = lambda b, i: (row0 + b * t) // tm + i
    return pl.pallas_call(
        lambda w, p, xn, y_any, o, cp, cn: _seq_dft_tiled_kernel(w, p, xn, o, cp, cn, scale=scale),
        grid=(nseq, t // tm, 2, groups),
        in_specs=[pl.BlockSpec((None, tm, t), lambda b, i, k, g: (k, i, 0)),
                  pl.BlockSpec((None, t, half), lambda b, i, k, g: (k, blk_t(b), g)),
                  pl.BlockSpec((t, NYQ_PAD), lambda b, i, k, g: (blk_t(b), g)),
                  pl.BlockSpec(memory_space=pl.ANY)],
        out_specs=pl.BlockSpec((2, tm, half), lambda b, i, k, g: (0, blk_m(b, i), g * k)),
        out_shape=jax.ShapeDtypeStruct((2, m, n), BF16),
        scratch_shapes=[pltpu.VMEM((groups, tm, half), F32),
                        pltpu.VMEM((groups, tm, NYQ_PAD), F32)],
        input_output_aliases={3: 0},
        compiler_params=_cparams(("parallel", "parallel", "arbitrary", "arbitrary"), 56),
        name="seq_dft_tiled",
    )(wt, planes, nyq, y_prev)


def _pos_embed_2d(n_tok, d):
    rows = n_tok // GRID_W
    quarter = d // 4
    omega = 1.0 / (10000.0 ** (jnp.arange(quarter, dtype=F32) / quarter))
    er = jnp.arange(rows, dtype=F32)[:, None] * omega
    ec = jnp.arange(GRID_W, dtype=F32)[:, None] * omega
    er = jnp.concatenate([jnp.sin(er), jnp.cos(er)], axis=-1)
    ec = jnp.concatenate([jnp.sin(ec), jnp.cos(ec)], axis=-1)
    half = d // 2
    emb = jnp.concatenate([jnp.broadcast_to(er[:, None, :], (rows, GRID_W, half)),
                           jnp.broadcast_to(ec[None, :, :], (rows, GRID_W, half))], axis=-1)
    return emb.reshape(rows * GRID_W, d)


def kernel(x_prompt, x_sample, state_fwd, state_bwd, c, c_ctx, norm1_g, norm2_g, w_mod, b_mod,
           gla_w_in, gla_wa1_f, gla_wa2_f, gla_ba_f, gla_wa1_b, gla_wa2_b, gla_ba_b,
           gla_onorm_g, gla_w_out, fnet_w, mlp_w1, mlp_w2, final_g):
    batch, seq, d = x_prompt.shape
    dec_batch, dec_seq, _ = x_sample.shape
    depth = w_mod.shape[0]
    n_p, n_l = batch * seq, dec_batch * dec_seq
    h = GLA_HEADS
    hk, hv = gla_wa2_f.shape[2] // h, gla_onorm_g.shape[1]
    rank = gla_wa1_f.shape[2]
    cg = d // F_GROUPS

    n_cond = -(-(1 + dec_batch) // 8) * 8
    cond = jnp.zeros((n_cond, d), F32).at[0].set(c_ctx).at[1:1 + dec_batch].set(c)
    mods = _modulation(cond, w_mod, b_mod)
    mods3 = mods.reshape(depth * n_cond * N_MOD, 1, d)
    rows = _Rows(n_p, dec_seq, n_cond)

    x, hb0 = _embed(x_prompt.reshape(n_p, d), x_sample.reshape(n_l, d), _pos_embed_2d(dec_seq, d),
                    norm1_g[0], mods3, rows)

    tables = _gla_tables(GLA_CHUNK)
    dft_c = _chan_table(cg)
    dft_tp = _dft_tables(seq)
    dft_tl = _dft_tables(dec_seq)
    w_in, w_out = gla_w_in.astype(BF16), gla_w_out.astype(BF16)
    w_fnet = _mirror_weight(fnet_w, cg)
    w2 = mlp_w2.astype(BF16)
    n_gla = gla_w_in.shape[0]
    wa1 = jnp.zeros((n_gla, d, A1_WIDTH), F32)
    wa1 = wa1.at[:, :, :rank].set(gla_wa1_f).at[:, :, rank:2 * rank].set(gla_wa1_b).astype(BF16)
    new_states = None
    for i in range(depth):
        j = i // 2
        hb = hb0 if i == 0 else _norm_mod(x, norm1_g[i], mods3, rows, i, 0)
        resid = lambda which: (x, mods3, rows, i, which)
        if i % 2 == 0:
            z = _matmul(hb, w_in, w_layer=j, out_dtype=F32)
            wa2 = jnp.zeros((2 * rank, 2 * h * hk), F32)
            wa2 = wa2.at[:rank, :h * hk].set(gla_wa2_f[j])
            wa2 = wa2.at[rank:2 * rank, h * hk:].set(gla_wa2_b[j])
            ba = jnp.concatenate([gla_ba_f[j], gla_ba_b[j]]).reshape(1, -1)
            g1, g2 = _log_gates(hb, wa1, j, wa2, ba)
            common = dict(hk=hk, hv=hv, layer=j, n_layers=n_gla)
            o_f, o_b, s_f, s_b = _gla_scan(z, g1, g2, tables, row0=0, nseq=batch, t=seq,
                                           emit_final=True, final=new_states, **common)
            new_states = (s_f, s_b)
            o_f, o_b = _gla_scan(z, g1, g2, tables, row0=n_p, nseq=dec_batch,
                                 t=dec_seq, prev=(o_f, o_b), init=(state_fwd, state_bwd),
                                 **common)
            y = _gla_out(o_f, o_b, z, gla_onorm_g[j], hv)
            x = _matmul(y, w_out, w_layer=j, out_dtype=F32, epilogue="resid", resid=resid(2))
        else:
            planes, nyq = _chan_dft(hb, dft_c)
            y = _seq_dft_whole(planes, nyq, dft_tp, row0=0, nseq=batch, t=seq, half=cg // 2,
                               scale=1.0 / math.sqrt(seq * cg))
            y = _seq_dft_tiled(planes, nyq, dft_tl, y, row0=n_p, nseq=dec_batch, t=dec_seq,
                               half=cg // 2, scale=1.0 / math.sqrt(dec_seq * cg))
            x = _matmul(y, w_fnet, w_layer=j, out_dtype=F32, epilogue="resid", resid=resid(2))
        hb = _norm_mod(x, norm2_g[i], mods3, rows, i, 3)
        a = _matmul(hb, mlp_w1, w_layer=i, out_dtype=BF16, epilogue="relu2", tm=2048, tn=512)
        x = _matmul(a, w2, w_layer=i, out_dtype=F32, epilogue="resid", resid=resid(5))
    y_prompt = _final_norm(x, final_g, 0, n_p).reshape(batch, seq, d)
    y_sample = _final_norm(x, final_g, n_p, n_l).reshape(dec_batch, dec_seq, d)
    return (y_prompt, y_sample) + tuple(new_states)
```
